```python
import math
import jax, jax.numpy as jnp
from jax import lax
import numpy as np

D_MODEL = 1024
BATCH = 16
SEQ = 2048
DEPTH = 1

CHUNK = 64
Q_BLOCK = 128
ATTN_HEADS = 4
HEAD_DIM = 64
ATTN_WIDTH = ATTN_HEADS * 2 * HEAD_DIM
CONV_CH = 512
CONV_WIDTH = 31
FFN_DIM = 2816
FFN_CONV_WIDTH = 3
LN_EPS = 1e-5
DEEPNORM_ALPHA = (2.0 * DEPTH) ** 0.25
DEEPNORM_BETA = (8.0 * DEPTH) ** -0.25
IN_COLS = 3 * ATTN_WIDTH + 2 * CONV_CH + 2 * D_MODEL

kernel_name = "diffattn_conformer_gated_hybrid_deepnorm"


def layer_norm(x, g, b):
    x32 = x.astype(jnp.float32)
    mu = jnp.mean(x32, axis=-1, keepdims=True)
    var = jnp.mean(jnp.square(x32 - mu), axis=-1, keepdims=True)
    y = (x32 - mu) * lax.rsqrt(var + LN_EPS) * g.astype(jnp.float32) + b.astype(jnp.float32)
    return y.astype(x.dtype)


def causal_dwconv(x, w, b):
    k_width, ch = w.shape
    y = lax.conv_general_dilated(
        x, w.astype(x.dtype)[:, None, :], window_strides=(1,), padding=[(k_width - 1, 0)],
        dimension_numbers=('NWC', 'WIO', 'NWC'), feature_group_count=ch)
    return y + b.astype(x.dtype)


def diff_attention(q, k, v, lam, lambda_init, sub_gain):
    bsz, seq = q.shape[0], q.shape[1]
    scale = HEAD_DIM ** -0.5
    slopes = jnp.exp2(-8.0 * jnp.arange(1, ATTN_HEADS + 1, dtype=jnp.float32) / ATTN_HEADS)
    kpos = jnp.arange(seq)

    def block(i):
        start = i * Q_BLOCK
        qb = lax.dynamic_slice_in_dim(q, start, Q_BLOCK, axis=1)
        qpos = start + jnp.arange(Q_BLOCK)
        s = jnp.einsum('bqhmd,bkhmd->bhmqk', qb, k).astype(jnp.float32) * scale
        dist = jnp.abs(qpos[:, None] - kpos[None, :]).astype(jnp.float32)
        bias = -slopes[:, None, None] * dist
        allowed = (kpos[None, :] // CHUNK) <= (qpos[:, None] // CHUNK)
        s = jnp.where(allowed, s + bias[None, :, None], -jnp.inf)
        p = jax.nn.softmax(s, axis=-1)
        a = p[:, :, 0] - lam * p[:, :, 1]
        o = jnp.einsum('bhqk,bkhe->bqhe', a.astype(v.dtype), v)
        o32 = o.astype(jnp.float32)
        o32 = o32 * lax.rsqrt(jnp.mean(jnp.square(o32), axis=-1, keepdims=True) + LN_EPS)
        o32 = o32 * sub_gain.astype(jnp.float32) * (1.0 - lambda_init)
        return o32.astype(v.dtype)

    out = lax.map(block, jnp.arange(seq // Q_BLOCK))
    out = jnp.moveaxis(out, 0, 1)
    return out.reshape(bsz, seq, ATTN_WIDTH)


def setup_inputs(seed: int = 0) -> dict:
    key = jax.random.key(seed)
    ks = jax.random.split(key, 24)
    L, D, A, C, F = DEPTH, D_MODEL, ATTN_WIDTH, CONV_CH, FFN_DIM
    beta = DEEPNORM_BETA
    nrm = lambda k, shape: jax.random.normal(k, shape, dtype=jnp.float32)
    col_scale = jnp.concatenate([
        jnp.ones((2 * A,), jnp.float32),
        jnp.full((A,), beta, jnp.float32),
        jnp.ones((2 * C + 2 * D,), jnp.float32),
    ])
    return {
        "x": nrm(ks[0], (BATCH, SEQ, D)),
        "w_in": nrm(ks[1], (L, D, IN_COLS)) * (D ** -0.5) * col_scale,
        "b_gate": 0.02 * nrm(ks[2], (L, 2 * D)),
        "lambda_q1": 0.1 * nrm(ks[3], (L, HEAD_DIM)),
        "lambda_k1": 0.1 * nrm(ks[4], (L, HEAD_DIM)),
        "lambda_q2": 0.1 * nrm(ks[5], (L, HEAD_DIM)),
        "lambda_k2": 0.1 * nrm(ks[6], (L, HEAD_DIM)),
        "subln_gain": 1.0 + 0.02 * nrm(ks[7], (L, 2 * HEAD_DIM)),
        "w_attn_proj": nrm(ks[8], (L, A, D)) * (A ** -0.5) * beta,
        "conv_dw_w": nrm(ks[9], (L, CONV_WIDTH, C)) * (CONV_WIDTH ** -0.5),
        "conv_dw_b": 0.02 * nrm(ks[10], (L, C)),
        "conv_ln_g": 1.0 + 0.02 * nrm(ks[11], (L, C)),
        "conv_ln_b": 0.02 * nrm(ks[12], (L, C)),
        "w_conv_proj": nrm(ks[13], (L, C, D)) * (C ** -0.5) * beta,
        "b_conv_proj": 0.02 * nrm(ks[14], (L, D)),
        "w_out": nrm(ks[15], (L, D, D)) * (D ** -0.5) * beta,
        "ln1_g": 1.0 + 0.02 * nrm(ks[16], (L, D)),
        "ln1_b": 0.02 * nrm(ks[17], (L, D)),
        "w_ffn_in": nrm(ks[18], (L, D, 2 * F)) * (D ** -0.5) * beta,
        "ffn_dw_w": nrm(ks[19], (L, FFN_CONV_WIDTH, F)) * (FFN_CONV_WIDTH ** -0.5),
        "ffn_dw_b": 0.02 * nrm(ks[20], (L, F)),
        "w_ffn_down": nrm(ks[21], (L, F, D)) * (F ** -0.5) * beta,
        "ln2_g": 1.0 + 0.02 * nrm(ks[22], (L, D)),
        "ln2_b": 0.02 * nrm(ks[23], (L, D)),
    }


def reference(x, w_in, b_gate, lambda_q1, lambda_k1, lambda_q2, lambda_k2, subln_gain,
              w_attn_proj, conv_dw_w, conv_dw_b, conv_ln_g, conv_ln_b, w_conv_proj, b_conv_proj,
              w_out, ln1_g, ln1_b, w_ffn_in, ffn_dw_w, ffn_dw_b, w_ffn_down, ln2_g, ln2_b):
    bsz, seq = x.shape[0], x.shape[1]
    A, C, D, F = ATTN_WIDTH, CONV_CH, D_MODEL, FFN_DIM
    for l in range(DEPTH):
        lambda_init = 0.8 - 0.6 * math.exp(-0.3 * l)
        proj = jnp.einsum('bsd,dc->bsc', x, w_in[l])
        q, k, v, glu, g = jnp.split(proj, [A, 2 * A, 3 * A, 3 * A + 2 * C], axis=-1)
        q = q.reshape(bsz, seq, ATTN_HEADS, 2, HEAD_DIM)
        k = k.reshape(bsz, seq, ATTN_HEADS, 2, HEAD_DIM)
        v = v.reshape(bsz, seq, ATTN_HEADS, 2 * HEAD_DIM)
        gates = jax.nn.sigmoid((g + b_gate[l]).astype(jnp.float32)).astype(x.dtype)
        g_attn, g_conv = jnp.split(gates, 2, axis=-1)

        lam = (jnp.exp(jnp.sum(lambda_q1[l].astype(jnp.float32) * lambda_k1[l].astype(jnp.float32)))
               - jnp.exp(jnp.sum(lambda_q2[l].astype(jnp.float32) * lambda_k2[l].astype(jnp.float32)))
               + lambda_init)
        attn = diff_attention(q, k, v, lam, lambda_init, subln_gain[l])
        attn = jnp.einsum('bsa,ad->bsd', attn, w_attn_proj[l])

        u = glu[..., :C] * jax.nn.sigmoid(glu[..., C:])
        u = causal_dwconv(u, conv_dw_w[l], conv_dw_b[l])
        u = jax.nn.silu(layer_norm(u, conv_ln_g[l], conv_ln_b[l]))
        conv = jnp.einsum('bsc,cd->bsd', u, w_conv_proj[l]) + b_conv_proj[l]

        mixed = g_attn * attn + g_conv * conv
        y = jnp.einsum('bsd,de->bse', mixed, w_out[l])
        x = layer_norm(DEEPNORM_ALPHA * x + y, ln1_g[l], ln1_b[l])

        up = jnp.einsum('bsd,df->bsf', x, w_ffn_in[l])
        gate, val = jnp.split(up, [F], axis=-1)
        gate = causal_dwconv(gate, ffn_dw_w[l], ffn_dw_b[l])
        hid = jax.nn.gelu(gate, approximate=False) * val
        f = jnp.einsum('bsf,fd->bsd', hid, w_ffn_down[l])
        x = layer_norm(DEEPNORM_ALPHA * x + f, ln2_g[l], ln2_b[l])
    return x
```

```python
import functools
import math

import jax
import jax.numpy as jnp
from jax import lax
from jax.experimental import pallas as pl
from jax.experimental.pallas import tpu as pltpu

CHUNK = 64
ATTN_HEADS = 4
HEAD_DIM = 64
HEAD_WIDTH = 2 * HEAD_DIM
LN_EPS = 1e-5

LANES = 128
Q_TILE = 256
TOKEN_TILE = 512
HALO = 32
CARRY = 8
VMEM_LIMIT = 56 * 1024 * 1024

BF16 = jnp.bfloat16
F32 = jnp.float32


def _dot(a, b):
    return jnp.dot(a, b, preferred_element_type=F32)


def _dot_nt(a, b):
    return lax.dot_general(a, b, (((1,), (1,)), ((), ())), preferred_element_type=F32)


def _layer_norm(x, g, b):
    mu = jnp.mean(x, axis=-1, keepdims=True)
    xc = x - mu
    var = jnp.mean(xc * xc, axis=-1, keepdims=True)
    return xc * lax.rsqrt(var + LN_EPS) * g + b


def _const_spec(shape):
    nd = len(shape)
    return pl.BlockSpec(shape, lambda *_: (0,) * nd, pipeline_mode=pl.Buffered(1))


def _in_proj_kernel(x_ref, w_ref, q_ref, k_ref, v_ref, u_ref, *, width, scale):
    xb = x_ref[...].astype(BF16)
    q_ref[...] = (_dot(xb, w_ref[:, 0:width]) * scale).astype(BF16)
    k_ref[...] = _dot(xb, w_ref[:, width:2 * width]).astype(BF16)
    v_ref[...] = _dot(xb, w_ref[:, 2 * width:3 * width]).astype(BF16)
    a = _dot(xb, w_ref[:, 3 * width:4 * width])
    g = _dot(xb, w_ref[:, 4 * width:5 * width])
    u_ref[...] = (a * jax.nn.sigmoid(g)).astype(BF16)


def _in_proj(x2d, w, width):
    m, d = x2d.shape
    tm = TOKEN_TILE
    out = jax.ShapeDtypeStruct((m, width), BF16)
    row_spec = pl.BlockSpec((tm, width), lambda i: (i, 0))
    return pl.pallas_call(
        functools.partial(_in_proj_kernel, width=width, scale=HEAD_DIM ** -0.5),
        grid=(m // tm,),
        in_specs=[pl.BlockSpec((tm, d), lambda i: (i, 0)), _const_spec(w.shape)],
        out_specs=[row_spec] * 4,
        out_shape=[out] * 4,
        compiler_params=pltpu.CompilerParams(
            dimension_semantics=("arbitrary",), vmem_limit_bytes=VMEM_LIMIT),
        name="in_proj",
    )(x2d, w)


def _attn_kernel(lam_ref, gain_ref, q_ref, k_ref, v_ref, o_ref,
                 qe0_ref, qe1_ref, ke_ref, *, seq, lambda_init):
    h = pl.program_id(1)
    slope = jnp.exp2(jnp.full((1, 1), -8.0 / ATTN_HEADS, F32) * (h + 1).astype(F32))

    lp = lam_ref[...]
    dot1 = jnp.sum(lp[0:1, :] * lp[1:2, :], axis=-1, keepdims=True)
    dot2 = jnp.sum(lp[2:3, :] * lp[3:4, :], axis=-1, keepdims=True)
    lam = jnp.exp(dot1) - jnp.exp(dot2) + lambda_init

    row = lax.broadcasted_iota(jnp.int32, (seq, LANES), 0)
    lane = lax.broadcasted_iota(jnp.int32, (seq, LANES), 1)
    hi = (row & ~255).astype(F32) * slope
    lo = (row & 255).astype(F32) * slope
    one = jnp.where(lane < 4, 1.0, 0.0)
    q_aug = jnp.where(lane == 0, -hi, jnp.where(lane == 1, -lo, one)).astype(BF16)
    k_aug = jnp.where(lane == 2, hi, jnp.where(lane == 3, lo, one)).astype(BF16)

    q = q_ref[...]
    zero = jnp.zeros_like(q)
    qe0_ref[:, 0:LANES] = jnp.where(lane < HEAD_DIM, q, zero)
    qe0_ref[:, LANES:] = q_aug
    qe1_ref[:, 0:LANES] = jnp.where(lane >= HEAD_DIM, q, zero)
    qe1_ref[:, LANES:] = q_aug
    ke_ref[:, 0:LANES] = k_ref[...]
    ke_ref[:, LANES:] = k_aug

    r = lax.broadcasted_iota(jnp.int32, (Q_TILE, Q_TILE), 0)
    c = lax.broadcasted_iota(jnp.int32, (Q_TILE, Q_TILE), 1)
    ahead = jnp.maximum(c - r, 0).astype(F32)
    diag_fix = jnp.where((c // CHUNK) <= (r // CHUNK), -2.0 * slope * ahead, -jnp.inf)

    gain = gain_ref[...] * (1.0 - lambda_init)

    for i in range(seq // Q_TILE):
        r0 = i * Q_TILE
        outs = []
        for qe_ref in (qe0_ref, qe1_ref):
            qb = qe_ref[r0:r0 + Q_TILE, :]
            s_d = _dot_nt(qb, ke_ref[r0:r0 + Q_TILE, :]) + diag_fix
            mx = jnp.max(s_d, axis=-1, keepdims=True)
            if i > 0:
                s_o = _dot_nt(qb, ke_ref[0:r0, :])
                mx = jnp.maximum(mx, jnp.max(s_o, axis=-1, keepdims=True))
            e_d = jnp.exp(s_d - mx)
            den = jnp.sum(e_d, axis=-1, keepdims=True)
            acc = _dot(e_d.astype(BF16), v_ref[r0:r0 + Q_TILE, :])
            if i > 0:
                e_o = jnp.exp(s_o - mx)
                den = den + jnp.sum(e_o, axis=-1, keepdims=True)
                acc = acc + _dot(e_o.astype(BF16), v_ref[0:r0, :])
            outs.append((acc, den))
        (a0, d0), (a1, d1) = outs
        o = a0 * (1.0 / d0) - a1 * (lam / d1)
        o = o * lax.rsqrt(jnp.mean(o * o, axis=-1, keepdims=True) + LN_EPS) * gain
        o_ref[r0:r0 + Q_TILE, :] = o.astype(BF16)


def _diff_attn(q, k, v, lam_params, gain, lambda_init):
    bsz, seq, width = q.shape
    heads = width // HEAD_WIDTH
    blk = pl.BlockSpec((None, seq, HEAD_WIDTH), lambda b, h: (b, 0, h))
    return pl.pallas_call(
        functools.partial(_attn_kernel, seq=seq, lambda_init=lambda_init),
        grid=(bsz, heads),
        in_specs=[_const_spec(lam_params.shape), _const_spec(gain.shape), blk, blk, blk],
        out_specs=blk,
        out_shape=jax.ShapeDtypeStruct((bsz, seq, width), BF16),
        scratch_shapes=[pltpu.VMEM((seq, 2 * LANES), BF16)] * 3,
        compiler_params=pltpu.CompilerParams(
            dimension_semantics=("arbitrary", "arbitrary"), vmem_limit_bytes=VMEM_LIMIT),
        name="diff_attn",
    )(lam_params, gain, q, k, v)


def _mix_kernel(x_ref, u_ref, uh_ref, attn_ref, wg_ref, bg_ref, wa_ref, cw_ref, cb_ref,
                cg_ref, cbeta_ref, wc_ref, bc_ref, wo_ref, g1_ref, b1_ref, o_ref, ext_ref,
                *, alpha, conv_width):
    j = pl.program_id(1)
    tm, d = x_ref.shape

    halo = uh_ref[...].astype(F32)
    ext_ref[0:HALO, :] = jnp.where(j > 0, halo, jnp.zeros_like(halo))
    ext_ref[HALO:, :] = u_ref[...].astype(F32)
    base = HALO - (conv_width - 1)
    acc = ext_ref[base:base + tm, :] * cw_ref[0:1, :] + cb_ref[...]
    for t in range(1, conv_width):
        acc = acc + ext_ref[base + t:base + t + tm, :] * cw_ref[t:t + 1, :]
    z = _layer_norm(acc, cg_ref[...], cbeta_ref[...])
    z = z * jax.nn.sigmoid(z)
    conv = _dot(z.astype(BF16), wc_ref[...]) + bc_ref[...]

    attn = _dot(attn_ref[...], wa_ref[...])

    x = x_ref[...]
    gates = jax.nn.sigmoid(_dot(x.astype(BF16), wg_ref[...]) + bg_ref[...])
    mixed = gates[:, 0:d] * attn + gates[:, d:] * conv
    y = _dot(mixed.astype(BF16), wo_ref[...])
    o_ref[...] = _layer_norm(alpha * x + y, g1_ref[...], b1_ref[...])


def _mix(x, u, attn, wg, bg, wa, cw, cb, cg, cbeta, wc, bc, wo, g1, b1, alpha):
    bsz, seq, d = x.shape
    ch = u.shape[-1]
    tm = TOKEN_TILE
    halo_blocks = tm // HALO
    consts = (wg, bg, wa, cw, cb, cg, cbeta, wc, bc, wo, g1, b1)
    return pl.pallas_call(
        functools.partial(_mix_kernel, alpha=alpha, conv_width=cw.shape[0]),
        grid=(bsz, seq // tm),
        in_specs=[
            pl.BlockSpec((None, tm, d), lambda b, j: (b, j, 0)),
            pl.BlockSpec((None, tm, ch), lambda b, j: (b, j, 0)),
            pl.BlockSpec((None, HALO, ch),
                         lambda b, j: (b, jnp.maximum(j * halo_blocks - 1, 0), 0)),
            pl.BlockSpec((None, tm, attn.shape[-1]), lambda b, j: (b, j, 0)),
        ] + [_const_spec(c.shape) for c in consts],
        out_specs=pl.BlockSpec((None, tm, d), lambda b, j: (b, j, 0)),
        out_shape=jax.ShapeDtypeStruct((bsz, seq, d), F32),
        scratch_shapes=[pltpu.VMEM((tm + HALO, ch), F32)],
        compiler_params=pltpu.CompilerParams(
            dimension_semantics=("arbitrary", "arbitrary"), vmem_limit_bytes=VMEM_LIMIT),
        name="mix",
    )(x, u, u, attn, *consts)


def _ffn_kernel(x_ref, wgate_ref, wval_ref, dw_ref, db_ref, wdown_ref, g2_ref, b2_ref,
                o_ref, ext_ref, carry_ref, hid_ref, *, alpha, conv_width, f_tile):
    j = pl.program_id(1)
    tm, _ = x_ref.shape
    f_dim = hid_ref.shape[-1]

    @pl.when(j == 0)
    def _():
        carry_ref[...] = jnp.zeros_like(carry_ref)

    x = x_ref[...]
    xb = x.astype(BF16)
    base = CARRY - (conv_width - 1)
    for c0 in range(0, f_dim, f_tile):
        cols = slice(c0, c0 + f_tile)
        gate = _dot(xb, wgate_ref[:, cols])
        ext_ref[0:CARRY, :] = carry_ref[:, cols]
        ext_ref[CARRY:, :] = gate
        carry_ref[:, cols] = gate[tm - CARRY:, :]
        acc = ext_ref[base:base + tm, :] * dw_ref[0:1, cols] + db_ref[:, cols]
        for t in range(1, conv_width):
            acc = acc + ext_ref[base + t:base + t + tm, :] * dw_ref[t:t + 1, cols]
        act = 0.5 * acc * (1.0 + lax.erf(acc * math.sqrt(0.5)))
        val = _dot(xb, wval_ref[:, cols])
        hid_ref[:, cols] = (act * val).astype(BF16)
    f = _dot(hid_ref[...], wdown_ref[...])
    o_ref[...] = _layer_norm(alpha * x + f, g2_ref[...], b2_ref[...])


def _ffn(x, wgate, wval, dw, db, wdown, g2, b2, alpha):
    bsz, seq, d = x.shape
    f_dim = wgate.shape[-1]
    tm = TOKEN_TILE
    f_tile = 2 * LANES
    consts = (wgate, wval, dw, db, wdown, g2, b2)
    return pl.pallas_call(
        functools.partial(_ffn_kernel, alpha=alpha, conv_width=dw.shape[0], f_tile=f_tile),
        grid=(bsz, seq // tm),
        in_specs=[pl.BlockSpec((None, tm, d), lambda b, j: (b, j, 0))]
        + [_const_spec(c.shape) for c in consts],
        out_specs=pl.BlockSpec((None, tm, d), lambda b, j: (b, j, 0)),
        out_shape=jax.ShapeDtypeStruct((bsz, seq, d), F32),
        scratch_shapes=[
            pltpu.VMEM((tm + CARRY, f_tile), F32),
            pltpu.VMEM((CARRY, f_dim), F32),
            pltpu.VMEM((tm, f_dim), BF16),
        ],
        compiler_params=pltpu.CompilerParams(
            dimension_semantics=("arbitrary", "arbitrary"), vmem_limit_bytes=VMEM_LIMIT),
        name="ffn",
    )(x, *consts)


def kernel(x, w_in, b_gate, lambda_q1, lambda_k1, lambda_q2, lambda_k2, subln_gain, w_attn_proj, conv_dw_w, conv_dw_b, conv_ln_g, conv_ln_b, w_conv_proj, b_conv_proj, w_out, ln1_g, ln1_b, w_ffn_in, ffn_dw_w, ffn_dw_b, w_ffn_down, ln2_g, ln2_b):
    bsz, seq, d = x.shape
    depth = w_in.shape[0]
    width = w_attn_proj.shape[1]
    ch = w_conv_proj.shape[1]
    f_dim = w_ffn_down.shape[1]
    assert width == ATTN_HEADS * HEAD_WIDTH and ch == width
    assert seq % TOKEN_TILE == 0 and seq % Q_TILE == 0 and f_dim % (2 * LANES) == 0
    alpha = (2.0 * depth) ** 0.25
    row = lambda a: a.reshape(1, -1)

    for l in range(depth):
        lambda_init = 0.8 - 0.6 * math.exp(-0.3 * l)
        w_l = w_in[l].astype(BF16)
        n_proj = 3 * width + 2 * ch
        q, k, v, u = _in_proj(x.reshape(bsz * seq, d), w_l[:, :n_proj], width)
        shape3 = (bsz, seq, width)
        lam_params = jnp.stack([lambda_q1[l], lambda_k1[l], lambda_q2[l], lambda_k2[l]])
        attn = _diff_attn(q.reshape(shape3), k.reshape(shape3), v.reshape(shape3),
                          lam_params, row(subln_gain[l]), lambda_init)
        x = _mix(x, u.reshape(bsz, seq, ch), attn,
                 w_l[:, n_proj:], row(b_gate[l]), w_attn_proj[l].astype(BF16),
                 conv_dw_w[l], row(conv_dw_b[l]), row(conv_ln_g[l]), row(conv_ln_b[l]),
                 w_conv_proj[l].astype(BF16), row(b_conv_proj[l]), w_out[l].astype(BF16),
                 row(ln1_g[l]), row(ln1_b[l]), alpha)
        w_up = w_ffn_in[l].astype(BF16)
        x = _ffn(x, w_up[:, :f_dim], w_up[:, f_dim:], ffn_dw_w[l], row(ffn_dw_b[l]),
                 w_ffn_down[l].astype(BF16), row(ln2_g[l]), row(ln2_b[l]), alpha)
    return x
```

```python
import functools
import math

import jax
import jax.numpy as jnp
from jax import lax
from jax.experimental import pallas as pl
from jax.experimental.pallas import tpu as pltpu

CHUNK = 64
ATTN_HEADS = 4
HEAD_DIM = 64
HEAD_WIDTH = 2 * HEAD_DIM
LN_EPS = 1e-5

LANES = 128
Q_TILE = 256
TOKEN_TILE = 512
HALO = 32
CARRY = 8
CONV_ROWS = 128
VMEM_LIMIT = 56 * 1024 * 1024

BF16 = jnp.bfloat16
F32 = jnp.float32


def _dot(a, b):
    return jnp.dot(a, b, preferred_element_type=F32)


def _dot_nt(a, b):
    return lax.dot_general(a, b, (((1,), (1,)), ((), ())), preferred_element_type=F32)


def _layer_norm(x, g, b):
    mu = jnp.mean(x, axis=-1, keepdims=True)
    xc = x - mu
    var = jnp.mean(xc * xc, axis=-1, keepdims=True)
    return xc * lax.rsqrt(var + LN_EPS) * g + b


def _const_spec(shape):
    nd = len(shape)
    return pl.BlockSpec(shape, lambda *_: (0,) * nd, pipeline_mode=pl.Buffered(1))


def _in_proj_kernel(x_ref, w_ref, q0_ref, q1_ref, k_ref, v_ref, u_ref, *, width, scale):
    xb = x_ref[...].astype(BF16)
    q = (_dot(xb, w_ref[:, 0:width]) * scale).astype(BF16)
    lane = lax.broadcasted_iota(jnp.int32, q.shape, 1)
    first = (lane & (HEAD_WIDTH - 1)) < HEAD_DIM
    zero = jnp.zeros_like(q)
    q0_ref[...] = jnp.where(first, q, zero)
    q1_ref[...] = jnp.where(first, zero, q)
    k_ref[...] = _dot(xb, w_ref[:, width:2 * width]).astype(BF16)
    v_ref[...] = _dot(xb, w_ref[:, 2 * width:3 * width]).astype(BF16)
    a = _dot(xb, w_ref[:, 3 * width:4 * width])
    g = _dot(xb, w_ref[:, 4 * width:5 * width])
    u_ref[...] = (a * jax.nn.sigmoid(g)).astype(BF16)


def _in_proj(x2d, w, width):
    m, d = x2d.shape
    tm = TOKEN_TILE
    out = jax.ShapeDtypeStruct((m, width), BF16)
    row_spec = pl.BlockSpec((tm, width), lambda i: (i, 0))
    return pl.pallas_call(
        functools.partial(_in_proj_kernel, width=width, scale=HEAD_DIM ** -0.5),
        grid=(m // tm,),
        in_specs=[pl.BlockSpec((tm, d), lambda i: (i, 0)), _const_spec(w.shape)],
        out_specs=[row_spec] * 5,
        out_shape=[out] * 5,
        compiler_params=pltpu.CompilerParams(
            dimension_semantics=("arbitrary",), vmem_limit_bytes=VMEM_LIMIT),
        name="in_proj",
    )(x2d, w)


def _head_slope(h):
    return jnp.exp2(jnp.full((1, 1), -8.0 / ATTN_HEADS, F32) * (h + 1).astype(F32))


def _alibi_kernel(qaug_ref, kaug_ref):
    slope = _head_slope(pl.program_id(0))
    row = lax.broadcasted_iota(jnp.int32, qaug_ref.shape, 0)
    lane = lax.broadcasted_iota(jnp.int32, qaug_ref.shape, 1)
    hi = (row & ~255).astype(F32)
    lo = (row & 255).astype(F32)
    tail = jnp.where(jnp.logical_and(lane >= 2, lane < 4), slope, 0.0)
    qaug_ref[...] = jnp.where(lane == 0, -hi * slope,
                              jnp.where(lane == 1, -lo * slope, tail)).astype(BF16)
    one = jnp.where(lane < 2, 1.0, 0.0)
    kaug_ref[...] = jnp.where(lane == 2, hi, jnp.where(lane == 3, lo, one)).astype(BF16)


def _alibi_tables(heads, seq):
    return pl.pallas_call(
        _alibi_kernel,
        grid=(heads,),
        out_specs=[pl.BlockSpec((None, seq, LANES), lambda h: (h, 0, 0)),
                   pl.BlockSpec((seq, LANES), lambda h: (0, 0))],
        out_shape=[jax.ShapeDtypeStruct((heads, seq, LANES), BF16),
                   jax.ShapeDtypeStruct((seq, LANES), BF16)],
        compiler_params=pltpu.CompilerParams(dimension_semantics=("arbitrary",)),
        name="alibi_tables",
    )()


def _attn_kernel(lam_ref, gain_ref, qaug_ref, kaug_ref, q0_ref, q1_ref, k_ref, v_ref, o_ref,
                 *, seq, lambda_init):
    slope = _head_slope(pl.program_id(0))

    lp = lam_ref[...]
    dot1 = jnp.sum(lp[0:1, :] * lp[1:2, :], axis=-1, keepdims=True)
    dot2 = jnp.sum(lp[2:3, :] * lp[3:4, :], axis=-1, keepdims=True)
    lam = jnp.exp(dot1) - jnp.exp(dot2) + lambda_init
    gain = gain_ref[...] * (1.0 - lambda_init)

    r = lax.broadcasted_iota(jnp.int32, (Q_TILE, Q_TILE), 0)
    c = lax.broadcasted_iota(jnp.int32, (Q_TILE, Q_TILE), 1)
    ahead = jnp.maximum(c - r, 0).astype(F32)
    diag_fix = jnp.where((c // CHUNK) <= (r // CHUNK), -2.0 * slope * ahead, -jnp.inf)

    def scores(i, q_ref):
        rows, keys = slice(i * Q_TILE, (i + 1) * Q_TILE), slice(0, (i + 1) * Q_TILE)
        lhs = jnp.concatenate([q_ref[rows, :], qaug_ref[rows, :]], axis=1)
        rhs = jnp.concatenate([k_ref[keys, :], kaug_ref[keys, :]], axis=1)
        s = _dot_nt(lhs, rhs)
        s_d = s[:, i * Q_TILE:] + diag_fix
        return s_d if i == 0 else jnp.concatenate([s[:, :i * Q_TILE], s_d], axis=1)

    def weighted_values(i, s):
        e = jnp.exp(s - jnp.max(s, axis=-1, keepdims=True))
        keys = (i + 1) * Q_TILE
        v_one = jnp.concatenate([v_ref[0:keys, :], jnp.ones((keys, LANES), BF16)], axis=1)
        return _dot(e.astype(BF16), v_one)

    n_blk = seq // Q_TILE
    order = [i for pair in zip(reversed(range(n_blk)), range(n_blk)) for i in pair][:n_blk]
    items = [(i, q_ref) for i in order for q_ref in (q0_ref, q1_ref)]
    s_next = scores(*items[0])
    pending = None
    for n, (i, _) in enumerate(items):
        s_cur = s_next
        if n + 1 < len(items):
            s_next = scores(*items[n + 1])
        acc = weighted_values(i, s_cur)
        if pending is None:
            pending = acc
            continue
        a0, a1 = pending, acc
        pending = None
        d0 = a0[:, HEAD_WIDTH:HEAD_WIDTH + 1]
        d1 = a1[:, HEAD_WIDTH:HEAD_WIDTH + 1]
        o = a0[:, :HEAD_WIDTH] * (1.0 / d0) - a1[:, :HEAD_WIDTH] * (lam / d1)
        o = o * lax.rsqrt(jnp.mean(o * o, axis=-1, keepdims=True) + LN_EPS) * gain
        o_ref[i * Q_TILE:(i + 1) * Q_TILE, :] = o.astype(BF16)


def _diff_attn(q0, q1, k, v, lam_params, gain, lambda_init):
    bsz, seq, width = k.shape
    heads = width // HEAD_WIDTH
    q_aug, k_aug = _alibi_tables(heads, seq)
    blk = pl.BlockSpec((None, seq, HEAD_WIDTH), lambda h, b: (b, 0, h))
    return pl.pallas_call(
        functools.partial(_attn_kernel, seq=seq, lambda_init=lambda_init),
        grid=(heads, bsz),
        in_specs=[_const_spec(lam_params.shape), _const_spec(gain.shape),
                  pl.BlockSpec((None, seq, LANES), lambda h, b: (h, 0, 0)),
                  _const_spec(k_aug.shape), blk, blk, blk, blk],
        out_specs=blk,
        out_shape=jax.ShapeDtypeStruct((bsz, seq, width), BF16),
        compiler_params=pltpu.CompilerParams(
            dimension_semantics=("arbitrary", "arbitrary"), vmem_limit_bytes=VMEM_LIMIT),
        name="diff_attn",
    )(lam_params, gain, q_aug, k_aug, q0, q1, k, v)


def _mix_kernel(x_ref, u_ref, uh_ref, attn_ref, wg_ref, bg_ref, wa_ref, cw_ref, cb_ref,
                cg_ref, cbeta_ref, wc_ref, bc_ref, wo_ref, g1_ref, b1_ref, o_ref, ext_ref,
                *, alpha, conv_width):
    j = pl.program_id(1)
    tm, d = x_ref.shape

    x = x_ref[...]
    xb = x.astype(BF16)
    halo = uh_ref[...].astype(F32)
    halo = jnp.where(j > 0, halo, jnp.zeros_like(halo))
    base = HALO - (conv_width - 1)
    n_slab = ext_ref.shape[0]
    g_cols = wg_ref.shape[1] // n_slab
    slabs, gate_parts = [], []
    for s in range(n_slab):
        gsl = slice(s * g_cols, (s + 1) * g_cols)
        gate_parts.append(jax.nn.sigmoid(_dot(xb, wg_ref[:, gsl]) + bg_ref[:, gsl]))
        lanes = slice(s * LANES, (s + 1) * LANES)
        ext_ref[s, 0:HALO, :] = halo[:, lanes]
        ext_ref[s, HALO:, :] = u_ref[:, lanes].astype(F32)
        blocks = []
        for r0 in range(0, tm, CONV_ROWS):
            acc = ext_ref[s, base + r0:base + r0 + CONV_ROWS, :] * cw_ref[0:1, lanes] + cb_ref[:, lanes]
            for t in range(1, conv_width):
                acc = acc + (ext_ref[s, base + t + r0:base + t + r0 + CONV_ROWS, :]
                             * cw_ref[t:t + 1, lanes])
            blocks.append(acc)
        slabs.append(jnp.concatenate(blocks, axis=0))
    gates = jnp.concatenate(gate_parts, axis=1)
    attn = _dot(attn_ref[...], wa_ref[...])
    z = _layer_norm(jnp.concatenate(slabs, axis=1), cg_ref[...], cbeta_ref[...])
    z = z * jax.nn.sigmoid(z)
    conv = _dot(z.astype(BF16), wc_ref[...]) + bc_ref[...]
    mixed = gates[:, 0:d] * attn + gates[:, d:] * conv
    y = _dot(mixed.astype(BF16), wo_ref[...])
    o_ref[...] = _layer_norm(alpha * x + y, g1_ref[...], b1_ref[...])


def _mix(x, u, attn, wg, bg, wa, cw, cb, cg, cbeta, wc, bc, wo, g1, b1, alpha):
    bsz, seq, d = x.shape
    ch = u.shape[-1]
    tm = TOKEN_TILE
    halo_blocks = tm // HALO
    consts = (wg, bg, wa, cw, cb, cg, cbeta, wc, bc, wo, g1, b1)
    return pl.pallas_call(
        functools.partial(_mix_kernel, alpha=alpha, conv_width=cw.shape[0]),
        grid=(bsz, seq // tm),
        in_specs=[
            pl.BlockSpec((None, tm, d), lambda b, j: (b, j, 0)),
            pl.BlockSpec((None, tm, ch), lambda b, j: (b, j, 0)),
            pl.BlockSpec((None, HALO, ch),
                         lambda b, j: (b, jnp.maximum(j * halo_blocks - 1, 0), 0)),
            pl.BlockSpec((None, tm, attn.shape[-1]), lambda b, j: (b, j, 0)),
        ] + [_const_spec(c.shape) for c in consts],
        out_specs=pl.BlockSpec((None, tm, d), lambda b, j: (b, j, 0)),
        out_shape=jax.ShapeDtypeStruct((bsz, seq, d), F32),
        scratch_shapes=[pltpu.VMEM((ch // LANES, tm + HALO, LANES), F32)],
        compiler_params=pltpu.CompilerParams(
            dimension_semantics=("arbitrary", "arbitrary"), vmem_limit_bytes=VMEM_LIMIT),
        name="mix",
    )(x, u, u, attn, *consts)


def _ffn_kernel(x_ref, wgate_ref, wval_ref, dw_ref, db_ref, wdown_ref, g2_ref, b2_ref,
                o_ref, ext_ref, carry_ref, hid_ref, *, alpha, conv_width, f_tile):
    j = pl.program_id(1)
    tm, _ = x_ref.shape
    f_dim = hid_ref.shape[-1]

    @pl.when(j == 0)
    def _():
        carry_ref[...] = jnp.zeros_like(carry_ref)

    x = x_ref[...]
    xb = x.astype(BF16)
    base = CARRY - (conv_width - 1)
    for c0 in range(0, f_dim, f_tile):
        cols = slice(c0, c0 + f_tile)
        gate = _dot(xb, wgate_ref[:, cols])
        slabs = []
        for s in range(ext_ref.shape[0]):
            lanes = slice(c0 + s * LANES, c0 + (s + 1) * LANES)
            g_s = gate[:, s * LANES:(s + 1) * LANES]
            ext_ref[s, 0:CARRY, :] = carry_ref[:, lanes]
            ext_ref[s, CARRY:, :] = g_s
            carry_ref[:, lanes] = g_s[tm - CARRY:, :]
            a_s = ext_ref[s, base:base + tm, :] * dw_ref[0:1, lanes] + db_ref[:, lanes]
            for t in range(1, conv_width):
                a_s = a_s + ext_ref[s, base + t:base + t + tm, :] * dw_ref[t:t + 1, lanes]
            slabs.append(a_s)
        acc = jnp.concatenate(slabs, axis=1)
        act = 0.5 * acc * (1.0 + lax.erf(acc * math.sqrt(0.5)))
        val = _dot(xb, wval_ref[:, cols])
        hid_ref[:, cols] = (act * val).astype(BF16)
    f = _dot(hid_ref[...], wdown_ref[...])
    o_ref[...] = _layer_norm(alpha * x + f, g2_ref[...], b2_ref[...])


def _ffn(x, wgate, wval, dw, db, wdown, g2, b2, alpha):
    bsz, seq, d = x.shape
    f_dim = wgate.shape[-1]
    tm = TOKEN_TILE
    f_tile = 2 * LANES
    consts = (wgate, wval, dw, db, wdown, g2, b2)
    return pl.pallas_call(
        functools.partial(_ffn_kernel, alpha=alpha, conv_width=dw.shape[0], f_tile=f_tile),
        grid=(bsz, seq // tm),
        in_specs=[pl.BlockSpec((None, tm, d), lambda b, j: (b, j, 0))]
        + [_const_spec(c.shape) for c in consts],
        out_specs=pl.BlockSpec((None, tm, d), lambda b, j: (b, j, 0)),
        out_shape=jax.ShapeDtypeStruct((bsz, seq, d), F32),
        scratch_shapes=[
            pltpu.VMEM((f_tile // LANES, tm + CARRY, LANES), F32),
            pltpu.VMEM((CARRY, f_dim), F32),
            pltpu.VMEM((tm, f_dim), BF16),
        ],
        compiler_params=pltpu.CompilerParams(
            dimension_semantics=("arbitrary", "arbitrary"), vmem_limit_bytes=VMEM_LIMIT),
        name="ffn",
    )(x, *consts)


def kernel(x, w_in, b_gate, lambda_q1, lambda_k1, lambda_q2, lambda_k2, subln_gain, w_attn_proj, conv_dw_w, conv_dw_b, conv_ln_g, conv_ln_b, w_conv_proj, b_conv_proj, w_out, ln1_g, ln1_b, w_ffn_in, ffn_dw_w, ffn_dw_b, w_ffn_down, ln2_g, ln2_b):
    bsz, seq, d = x.shape
    depth = w_in.shape[0]
    width = w_attn_proj.shape[1]
    ch = w_conv_proj.shape[1]
    f_dim = w_ffn_down.shape[1]
    assert width == ATTN_HEADS * HEAD_WIDTH and ch == width
    assert seq % TOKEN_TILE == 0 and seq % Q_TILE == 0 and f_dim % (2 * LANES) == 0
    alpha = (2.0 * depth) ** 0.25
    row = lambda a: a.reshape(1, -1)

    for l in range(depth):
        lambda_init = 0.8 - 0.6 * math.exp(-0.3 * l)
        w_l = w_in[l].astype(BF16)
        n_proj = 3 * width + 2 * ch
        q0, q1, k, v, u = _in_proj(x.reshape(bsz * seq, d), w_l[:, :n_proj], width)
        shape3 = (bsz, seq, width)
        lam_params = jnp.stack([lambda_q1[l], lambda_k1[l], lambda_q2[l], lambda_k2[l]])
        attn = _diff_attn(q0.reshape(shape3), q1.reshape(shape3), k.reshape(shape3),
                          v.reshape(shape3), lam_params, row(subln_gain[l]), lambda_init)
        x = _mix(x, u.reshape(bsz, seq, ch), attn,
                 w_l[:, n_proj:], row(b_gate[l]), w_attn_proj[l].astype(BF16),
                 conv_dw_w[l], row(conv_dw_b[l]), row(conv_ln_g[l]), row(conv_ln_b[l]),
                 w_conv_proj[l].astype(BF16), row(b_conv_proj[l]), w_out[l].astype(BF16),
                 row(ln1_g[l]), row(ln1_b[l]), alpha)
        w_up = w_ffn_in[l].astype(BF16)
        x = _ffn(x, w_up[:, :f_dim], w_up[:, f_dim:], ffn_dw_w[l], row(ffn_dw_b[l]),
                 w_ffn_down[l].astype(BF16), row(ln2_g[l]), row(ln2_b[l]), alpha)
    return x
```

```python
import functools
import math

import jax
import jax.numpy as jnp
from jax import lax
from jax.experimental import pallas as pl
from jax.experimental.pallas import tpu as pltpu

CHUNK = 64
ATTN_HEADS = 4
HEAD_DIM = 64
HEAD_WIDTH = 2 * HEAD_DIM
LN_EPS = 1e-5

LANES = 128
Q_TILE = 256
TOKEN_TILE = 512
HALO = 32
CARRY = 8
CONV_ROWS = 128
VMEM_LIMIT = 56 * 1024 * 1024

BF16 = jnp.bfloat16
F32 = jnp.float32


def _dot(a, b):
    return jnp.dot(a, b, preferred_element_type=F32)


def _dot_nt(a, b):
    return lax.dot_general(a, b, (((1,), (1,)), ((), ())), preferred_element_type=F32)


def _sigmoid(x):
    return 0.5 + 0.5 * jnp.tanh(0.5 * x)


def _layer_norm(x, g, b):
    mu = jnp.mean(x, axis=-1, keepdims=True)
    xc = x - mu
    var = jnp.mean(xc * xc, axis=-1, keepdims=True)
    return xc * lax.rsqrt(var + LN_EPS) * g + b


def _zero_after(x):
    bits = lax.bitcast_convert_type(x, jnp.uint32)
    return ((bits >> 16) >> 16).astype(F32)


def _const_spec(shape):
    nd = len(shape)
    return pl.BlockSpec(shape, lambda *_: (0,) * nd, pipeline_mode=pl.Buffered(1))


def _in_proj_kernel(x_ref, w_ref, q0_ref, q1_ref, k_ref, v_ref, u_ref, *, width, scale):
    xb = x_ref[...].astype(BF16)
    q = (_dot(xb, w_ref[:, 0:width]) * scale).astype(BF16)
    lane = lax.broadcasted_iota(jnp.int32, q.shape, 1)
    first = (lane & (HEAD_WIDTH - 1)) < HEAD_DIM
    zero = jnp.zeros_like(q)
    q0_ref[...] = jnp.where(first, q, zero)
    q1_ref[...] = jnp.where(first, zero, q)
    k_ref[...] = _dot(xb, w_ref[:, width:2 * width]).astype(BF16)
    v_ref[...] = _dot(xb, w_ref[:, 2 * width:3 * width]).astype(BF16)
    a = _dot(xb, w_ref[:, 3 * width:4 * width])
    g = _dot(xb, w_ref[:, 4 * width:5 * width])
    u_ref[...] = (a * _sigmoid(g)).astype(BF16)


def _in_proj(x2d, w, width):
    m, d = x2d.shape
    tm = TOKEN_TILE
    out = jax.ShapeDtypeStruct((m, width), BF16)
    row_spec = pl.BlockSpec((tm, width), lambda i: (i, 0))
    return pl.pallas_call(
        functools.partial(_in_proj_kernel, width=width, scale=HEAD_DIM ** -0.5),
        grid=(m // tm,),
        in_specs=[pl.BlockSpec((tm, d), lambda i: (i, 0)), _const_spec(w.shape)],
        out_specs=[row_spec] * 5,
        out_shape=[out] * 5,
        compiler_params=pltpu.CompilerParams(
            dimension_semantics=("arbitrary",), vmem_limit_bytes=VMEM_LIMIT),
        name="in_proj",
    )(x2d, w)


def _head_slope(h):
    return jnp.exp2(jnp.full((1, 1), -8.0 / ATTN_HEADS, F32) * (h + 1).astype(F32))


def _alibi_kernel(qaug_ref, kaug_ref):
    slope = _head_slope(pl.program_id(0))
    row = lax.broadcasted_iota(jnp.int32, qaug_ref.shape, 0)
    lane = lax.broadcasted_iota(jnp.int32, qaug_ref.shape, 1)
    hi = (row & ~255).astype(F32)
    lo = (row & 255).astype(F32)
    tail = jnp.where(jnp.logical_and(lane >= 2, lane < 4), slope, 0.0)
    qaug_ref[...] = jnp.where(lane == 0, -hi * slope,
                              jnp.where(lane == 1, -lo * slope, tail)).astype(BF16)
    one = jnp.where(lane < 2, 1.0, 0.0)
    kaug_ref[...] = jnp.where(lane == 2, hi, jnp.where(lane == 3, lo, one)).astype(BF16)


def _alibi_tables(heads, seq):
    return pl.pallas_call(
        _alibi_kernel,
        grid=(heads,),
        out_specs=[pl.BlockSpec((None, seq, LANES), lambda h: (h, 0, 0)),
                   pl.BlockSpec((seq, LANES), lambda h: (0, 0))],
        out_shape=[jax.ShapeDtypeStruct((heads, seq, LANES), BF16),
                   jax.ShapeDtypeStruct((seq, LANES), BF16)],
        compiler_params=pltpu.CompilerParams(dimension_semantics=("arbitrary",)),
        name="alibi_tables",
    )()


def _attn_kernel(lam_ref, gain_ref, qaug_ref, kaug_ref, q0_ref, q1_ref, k_ref, v_ref, o_ref,
                 *, seq, lambda_init):
    slope = _head_slope(pl.program_id(0))

    lp = lam_ref[...]
    dot1 = jnp.sum(lp[0:1, :] * lp[1:2, :], axis=-1, keepdims=True)
    dot2 = jnp.sum(lp[2:3, :] * lp[3:4, :], axis=-1, keepdims=True)
    lam = jnp.exp(dot1) - jnp.exp(dot2) + lambda_init
    gain = gain_ref[...] * (1.0 - lambda_init)

    r = lax.broadcasted_iota(jnp.int32, (Q_TILE, Q_TILE), 0)
    c = lax.broadcasted_iota(jnp.int32, (Q_TILE, Q_TILE), 1)
    ahead = jnp.maximum(c - r, 0).astype(F32)
    diag_fix = jnp.where((c // CHUNK) <= (r // CHUNK), -2.0 * slope * ahead, -jnp.inf)

    def scores(i, q_ref):
        rows, keys = slice(i * Q_TILE, (i + 1) * Q_TILE), slice(0, (i + 1) * Q_TILE)
        lhs = jnp.concatenate([q_ref[rows, :], qaug_ref[rows, :]], axis=1)
        rhs = jnp.concatenate([k_ref[keys, :], kaug_ref[keys, :]], axis=1)
        s = _dot_nt(lhs, rhs)
        s_d = s[:, i * Q_TILE:] + diag_fix
        return s_d if i == 0 else jnp.concatenate([s[:, :i * Q_TILE], s_d], axis=1)

    def weighted_values(i, s):
        e = jnp.exp(s - jnp.max(s, axis=-1, keepdims=True))
        keys = (i + 1) * Q_TILE
        v_one = jnp.concatenate([v_ref[0:keys, :], jnp.ones((keys, LANES), BF16)], axis=1)
        return _dot(e.astype(BF16), v_one)

    n_blk = seq // Q_TILE
    order = [i for pair in zip(reversed(range(n_blk)), range(n_blk)) for i in pair][:n_blk]
    items = [(i, q_ref) for i in order for q_ref in (q0_ref, q1_ref)]
    s_next = scores(*items[0])
    pending = None
    for n, (i, _) in enumerate(items):
        s_cur = s_next
        if n + 1 < len(items):
            s_next = scores(*items[n + 1])
        acc = weighted_values(i, s_cur)
        if pending is None:
            pending = acc
            continue
        a0, a1 = pending, acc
        pending = None
        d0 = a0[:, HEAD_WIDTH:HEAD_WIDTH + 1]
        d1 = a1[:, HEAD_WIDTH:HEAD_WIDTH + 1]
        o = a0[:, :HEAD_WIDTH] * (1.0 / d0) - a1[:, :HEAD_WIDTH] * (lam / d1)
        o = o * lax.rsqrt(jnp.mean(o * o, axis=-1, keepdims=True) + LN_EPS) * gain
        o_ref[i * Q_TILE:(i + 1) * Q_TILE, :] = o.astype(BF16)


def _diff_attn(q0, q1, k, v, lam_params, gain, lambda_init):
    bsz, seq, width = k.shape
    heads = width // HEAD_WIDTH
    q_aug, k_aug = _alibi_tables(heads, seq)
    blk = pl.BlockSpec((None, seq, HEAD_WIDTH), lambda h, b: (b, 0, h))
    return pl.pallas_call(
        functools.partial(_attn_kernel, seq=seq, lambda_init=lambda_init),
        grid=(heads, bsz),
        in_specs=[_const_spec(lam_params.shape), _const_spec(gain.shape),
                  pl.BlockSpec((None, seq, LANES), lambda h, b: (h, 0, 0)),
                  _const_spec(k_aug.shape), blk, blk, blk, blk],
        out_specs=blk,
        out_shape=jax.ShapeDtypeStruct((bsz, seq, width), BF16),
        compiler_params=pltpu.CompilerParams(
            dimension_semantics=("arbitrary", "arbitrary"), vmem_limit_bytes=VMEM_LIMIT),
        name="diff_attn",
    )(lam_params, gain, q_aug, k_aug, q0, q1, k, v)


def _mix_kernel(x_ref, u_ref, uh_ref, attn_ref, wg_ref, bg_ref, wa_ref, cw_ref, cb_ref,
                cg_ref, cbeta_ref, wc_ref, bc_ref, wo_ref, g1_ref, b1_ref, o_ref,
                ext_ref, xb_ref, gates_ref, attnp_ref, conv_ref, *, alpha, conv_width):
    j = pl.program_id(1)
    tm, d = x_ref.shape
    n_slab = ext_ref.shape[0]
    g_cols = wg_ref.shape[1] // n_slab
    a_cols = wa_ref.shape[1] // n_slab
    base = HALO - (conv_width - 1)
    xb_ref[...] = x_ref[...].astype(BF16)

    for s in range(n_slab):
        gsl = slice(s * g_cols, (s + 1) * g_cols)
        g = _dot(xb_ref[...], wg_ref[:, gsl])
        gates_ref[s] = g
        anchor = _zero_after(g[tm - 1:tm, g_cols - LANES:])
        lanes = slice(s * LANES, (s + 1) * LANES)
        halo = uh_ref[:, lanes].astype(F32)
        ext_ref[s, 0:HALO, :] = jnp.where(j > 0, halo, jnp.zeros_like(halo)) + anchor
        ext_ref[s, HALO:, :] = u_ref[:, lanes].astype(F32) + anchor
        for r0 in range(0, tm, CONV_ROWS):
            acc = ext_ref[s, base + r0:base + r0 + CONV_ROWS, :] * cw_ref[0:1, lanes] + cb_ref[:, lanes]
            for k in range(1, conv_width):
                acc = acc + (ext_ref[s, base + k + r0:base + k + r0 + CONV_ROWS, :]
                             * cw_ref[k:k + 1, lanes])
            conv_ref[s, r0:r0 + CONV_ROWS, :] = acc
    for s in range(n_slab):
        asl = slice(s * a_cols, (s + 1) * a_cols)
        attnp_ref[s] = _dot(attn_ref[...], wa_ref[:, asl])

    cat = lambda ref, lo, hi: jnp.concatenate([ref[s] for s in range(lo, hi)], axis=1)
    z = _layer_norm(cat(conv_ref, 0, n_slab), cg_ref[...], cbeta_ref[...])
    z = z * _sigmoid(z)
    conv = _dot(z.astype(BF16), wc_ref[...]) + bc_ref[...]
    half = n_slab // 2
    g_attn = _sigmoid(cat(gates_ref, 0, half) + bg_ref[:, 0:d])
    g_conv = _sigmoid(cat(gates_ref, half, n_slab) + bg_ref[:, d:])
    mixed = g_attn * cat(attnp_ref, 0, n_slab) + g_conv * conv
    y = _dot(mixed.astype(BF16), wo_ref[...])
    o_ref[...] = _layer_norm(alpha * x_ref[...] + y, g1_ref[...], b1_ref[...])


def _mix(x, u, attn, wg, bg, wa, cw, cb, cg, cbeta, wc, bc, wo, g1, b1, alpha):
    bsz, seq, d = x.shape
    ch = u.shape[-1]
    tm = TOKEN_TILE
    halo_blocks = tm // HALO
    n_slab = ch // LANES
    consts = (wg, bg, wa, cw, cb, cg, cbeta, wc, bc, wo, g1, b1)
    return pl.pallas_call(
        functools.partial(_mix_kernel, alpha=alpha, conv_width=cw.shape[0]),
        grid=(bsz, seq // tm),
        in_specs=[
            pl.BlockSpec((None, tm, d), lambda b, j: (b, j, 0)),
            pl.BlockSpec((None, tm, ch), lambda b, j: (b, j, 0)),
            pl.BlockSpec((None, HALO, ch),
                         lambda b, j: (b, jnp.maximum(j * halo_blocks - 1, 0), 0)),
            pl.BlockSpec((None, tm, attn.shape[-1]), lambda b, j: (b, j, 0)),
        ] + [_const_spec(c.shape) for c in consts],
        out_specs=pl.BlockSpec((None, tm, d), lambda b, j: (b, j, 0)),
        out_shape=jax.ShapeDtypeStruct((bsz, seq, d), F32),
        scratch_shapes=[
            pltpu.VMEM((n_slab, tm + HALO, LANES), F32),
            pltpu.VMEM((tm, d), BF16),
            pltpu.VMEM((n_slab, tm, wg.shape[1] // n_slab), F32),
            pltpu.VMEM((n_slab, tm, wa.shape[1] // n_slab), F32),
            pltpu.VMEM((n_slab, tm, LANES), F32),
        ],
        compiler_params=pltpu.CompilerParams(
            dimension_semantics=("arbitrary", "arbitrary"), vmem_limit_bytes=VMEM_LIMIT),
        name="mix",
    )(x, u, u, attn, *consts)


def _ffn_kernel(x_ref, wgate_ref, wval_ref, dw_ref, db_ref, wdown_ref, g2_ref, b2_ref,
                o_ref, ext_ref, carry_ref, hid_ref, *, alpha, conv_width, f_tile):
    j = pl.program_id(1)
    tm, _ = x_ref.shape
    f_dim = hid_ref.shape[-1]

    @pl.when(j == 0)
    def _():
        carry_ref[...] = jnp.zeros_like(carry_ref)

    x = x_ref[...]
    xb = x.astype(BF16)
    base = CARRY - (conv_width - 1)
    for c0 in range(0, f_dim, f_tile):
        cols = slice(c0, c0 + f_tile)
        gate = _dot(xb, wgate_ref[:, cols])
        slabs = []
        for s in range(ext_ref.shape[0]):
            lanes = slice(c0 + s * LANES, c0 + (s + 1) * LANES)
            g_s = gate[:, s * LANES:(s + 1) * LANES]
            ext_ref[s, 0:CARRY, :] = carry_ref[:, lanes]
            ext_ref[s, CARRY:, :] = g_s
            carry_ref[:, lanes] = g_s[tm - CARRY:, :]
            a_s = ext_ref[s, base:base + tm, :] * dw_ref[0:1, lanes] + db_ref[:, lanes]
            for t in range(1, conv_width):
                a_s = a_s + ext_ref[s, base + t:base + t + tm, :] * dw_ref[t:t + 1, lanes]
            slabs.append(a_s)
        acc = jnp.concatenate(slabs, axis=1)
        act = 0.5 * acc * (1.0 + lax.erf(acc * math.sqrt(0.5)))
        val = _dot(xb, wval_ref[:, cols])
        hid_ref[:, cols] = (act * val).astype(BF16)
    f = _dot(hid_ref[...], wdown_ref[...])
    o_ref[...] = _layer_norm(alpha * x + f, g2_ref[...], b2_ref[...])


def _ffn(x, wgate, wval, dw, db, wdown, g2, b2, alpha):
    bsz, seq, d = x.shape
    f_dim = wgate.shape[-1]
    tm = TOKEN_TILE
    f_tile = 2 * LANES
    consts = (wgate, wval, dw, db, wdown, g2, b2)
    return pl.pallas_call(
        functools.partial(_ffn_kernel, alpha=alpha, conv_width=dw.shape[0], f_tile=f_tile),
        grid=(bsz, seq // tm),
        in_specs=[pl.BlockSpec((None, tm, d), lambda b, j: (b, j, 0))]
        + [_const_spec(c.shape) for c in consts],
        out_specs=pl.BlockSpec((None, tm, d), lambda b, j: (b, j, 0)),
        out_shape=jax.ShapeDtypeStruct((bsz, seq, d), F32),
        scratch_shapes=[
            pltpu.VMEM((f_tile // LANES, tm + CARRY, LANES), F32),
            pltpu.VMEM((CARRY, f_dim), F32),
            pltpu.VMEM((tm, f_dim), BF16),
        ],
        compiler_params=pltpu.CompilerParams(
            dimension_semantics=("arbitrary", "arbitrary"), vmem_limit_bytes=VMEM_LIMIT),
        name="ffn",
    )(x, *consts)


def kernel(x, w_in, b_gate, lambda_q1, lambda_k1, lambda_q2, lambda_k2, subln_gain, w_attn_proj, conv_dw_w, conv_dw_b, conv_ln_g, conv_ln_b, w_conv_proj, b_conv_proj, w_out, ln1_g, ln1_b, w_ffn_in, ffn_dw_w, ffn_dw_b, w_ffn_down, ln2_g, ln2_b):
    bsz, seq, d = x.shape
    depth = w_in.shape[0]
    width = w_attn_proj.shape[1]
    ch = w_conv_proj.shape[1]
    f_dim = w_ffn_down.shape[1]
    assert width == ATTN_HEADS * HEAD_WIDTH and ch == width
    assert seq % TOKEN_TILE == 0 and seq % Q_TILE == 0 and f_dim % (2 * LANES) == 0
    alpha = (2.0 * depth) ** 0.25
    row = lambda a: a.reshape(1, -1)

    for l in range(depth):
        lambda_init = 0.8 - 0.6 * math.exp(-0.3 * l)
        w_l = w_in[l].astype(BF16)
        n_proj = 3 * width + 2 * ch
        q0, q1, k, v, u = _in_proj(x.reshape(bsz * seq, d), w_l[:, :n_proj], width)
        shape3 = (bsz, seq, width)
        lam_params = jnp.stack([lambda_q1[l], lambda_k1[l], lambda_q2[l], lambda_k2[l]])
        attn = _diff_attn(q0.reshape(shape3), q1.reshape(shape3), k.reshape(shape3),
                          v.reshape(shape3), lam_params, row(subln_gain[l]), lambda_init)
        x = _mix(x, u.reshape(bsz, seq, ch), attn,
                 w_l[:, n_proj:], row(b_gate[l]), w_attn_proj[l].astype(BF16),
                 conv_dw_w[l], row(conv_dw_b[l]), row(conv_ln_g[l]), row(conv_ln_b[l]),
                 w_conv_proj[l].astype(BF16), row(b_conv_proj[l]), w_out[l].astype(BF16),
                 row(ln1_g[l]), row(ln1_b[l]), alpha)
        w_up = w_ffn_in[l].astype(BF16)
        x = _ffn(x, w_up[:, :f_dim], w_up[:, f_dim:], ffn_dw_w[l], row(ffn_dw_b[l]),
                 w_ffn_down[l].astype(BF16), row(ln2_g[l]), row(ln2_b[l]), alpha)
    return x
```

```python
import functools
import math

import jax
import jax.numpy as jnp
from jax import lax
from jax.experimental import pallas as pl
from jax.experimental.pallas import tpu as pltpu

CHUNK = 64
ATTN_HEADS = 4
HEAD_DIM = 64
HEAD_WIDTH = 2 * HEAD_DIM
LN_EPS = 1e-5

LANES = 128
Q_TILE = 256
TOKEN_TILE = 512
WIDE_TILE = 1024
HALO = 32
CARRY = 8
CONV_ROWS = 128
VMEM_LIMIT = 56 * 1024 * 1024

BF16 = jnp.bfloat16
F32 = jnp.float32


def _dot(a, b):
    return jnp.dot(a, b, preferred_element_type=F32)


def _dot_nt(a, b):
    return lax.dot_general(a, b, (((1,), (1,)), ((), ())), preferred_element_type=F32)


def _sigmoid(x):
    return 0.5 + 0.5 * jnp.tanh(0.5 * x)


def _layer_norm(x, g, b):
    mu = jnp.mean(x, axis=-1, keepdims=True)
    xc = x - mu
    var = jnp.mean(xc * xc, axis=-1, keepdims=True)
    return xc * lax.rsqrt(var + LN_EPS) * g + b


def _zero_after(x):
    bits = lax.bitcast_convert_type(x, jnp.uint32)
    return ((bits >> 16) >> 16).astype(F32)


def _const_spec(shape):
    nd = len(shape)
    return pl.BlockSpec(shape, lambda *_: (0,) * nd, pipeline_mode=pl.Buffered(1))


def _in_proj_kernel(x_ref, w_ref, q0_ref, q1_ref, k_ref, v_ref, u_ref, *, width, scale):
    xb = x_ref[...].astype(BF16)
    q = (_dot(xb, w_ref[:, 0:width]) * scale).astype(BF16)
    lane = lax.broadcasted_iota(jnp.int32, q.shape, 1)
    first = (lane & (HEAD_WIDTH - 1)) < HEAD_DIM
    zero = jnp.zeros_like(q)
    q0_ref[...] = jnp.where(first, q, zero)
    q1_ref[...] = jnp.where(first, zero, q)
    k_ref[...] = _dot(xb, w_ref[:, width:2 * width]).astype(BF16)
    v_ref[...] = _dot(xb, w_ref[:, 2 * width:3 * width]).astype(BF16)
    a = _dot(xb, w_ref[:, 3 * width:4 * width])
    g = _dot(xb, w_ref[:, 4 * width:5 * width])
    u_ref[...] = (a * _sigmoid(g)).astype(BF16)


def _in_proj(x2d, w, width):
    m, d = x2d.shape
    tm = WIDE_TILE
    out = jax.ShapeDtypeStruct((m, width), BF16)
    row_spec = pl.BlockSpec((tm, width), lambda i: (i, 0))
    return pl.pallas_call(
        functools.partial(_in_proj_kernel, width=width, scale=HEAD_DIM ** -0.5),
        grid=(m // tm,),
        in_specs=[pl.BlockSpec((tm, d), lambda i: (i, 0)), _const_spec(w.shape)],
        out_specs=[row_spec] * 5,
        out_shape=[out] * 5,
        compiler_params=pltpu.CompilerParams(
            dimension_semantics=("arbitrary",), vmem_limit_bytes=VMEM_LIMIT),
        name="in_proj",
    )(x2d, w)


def _head_slope(h):
    return jnp.exp2(jnp.full((1, 1), -8.0 / ATTN_HEADS, F32) * (h + 1).astype(F32))


def _alibi_kernel(qaug_ref, kaug_ref):
    slope = _head_slope(pl.program_id(0))
    row = lax.broadcasted_iota(jnp.int32, qaug_ref.shape, 0)
    lane = lax.broadcasted_iota(jnp.int32, qaug_ref.shape, 1)
    hi = (row & ~255).astype(F32)
    lo = (row & 255).astype(F32)
    tail = jnp.where(jnp.logical_and(lane >= 2, lane < 4), slope, 0.0)
    qaug_ref[...] = jnp.where(lane == 0, -hi * slope,
                              jnp.where(lane == 1, -lo * slope, tail)).astype(BF16)
    one = jnp.where(lane < 2, 1.0, 0.0)
    kaug_ref[...] = jnp.where(lane == 2, hi, jnp.where(lane == 3, lo, one)).astype(BF16)


def _alibi_tables(heads, seq):
    return pl.pallas_call(
        _alibi_kernel,
        grid=(heads,),
        out_specs=[pl.BlockSpec((None, seq, LANES), lambda h: (h, 0, 0)),
                   pl.BlockSpec((seq, LANES), lambda h: (0, 0))],
        out_shape=[jax.ShapeDtypeStruct((heads, seq, LANES), BF16),
                   jax.ShapeDtypeStruct((seq, LANES), BF16)],
        compiler_params=pltpu.CompilerParams(dimension_semantics=("arbitrary",)),
        name="alibi_tables",
    )()


def _attn_kernel(lam_ref, gain_ref, qaug_ref, kaug_ref, q0_ref, q1_ref, k_ref, v_ref, o_ref,
                 *, seq, lambda_init):
    slope = _head_slope(pl.program_id(0))

    lp = lam_ref[...]
    dot1 = jnp.sum(lp[0:1, :] * lp[1:2, :], axis=-1, keepdims=True)
    dot2 = jnp.sum(lp[2:3, :] * lp[3:4, :], axis=-1, keepdims=True)
    lam = jnp.exp(dot1) - jnp.exp(dot2) + lambda_init
    gain = gain_ref[...] * (1.0 - lambda_init)

    r = lax.broadcasted_iota(jnp.int32, (Q_TILE, Q_TILE), 0)
    c = lax.broadcasted_iota(jnp.int32, (Q_TILE, Q_TILE), 1)
    ahead = jnp.maximum(c - r, 0).astype(F32)
    diag_fix = jnp.where((c // CHUNK) <= (r // CHUNK), -2.0 * slope * ahead, -jnp.inf)

    def scores(i, q_ref):
        rows, keys = slice(i * Q_TILE, (i + 1) * Q_TILE), slice(0, (i + 1) * Q_TILE)
        lhs = jnp.concatenate([q_ref[rows, :], qaug_ref[rows, :]], axis=1)
        rhs = jnp.concatenate([k_ref[keys, :], kaug_ref[keys, :]], axis=1)
        s = _dot_nt(lhs, rhs)
        s_d = s[:, i * Q_TILE:] + diag_fix
        return s_d if i == 0 else jnp.concatenate([s[:, :i * Q_TILE], s_d], axis=1)

    def weighted_values(i, s):
        e = jnp.exp(s - jnp.max(s, axis=-1, keepdims=True))
        keys = (i + 1) * Q_TILE
        v_one = jnp.concatenate([v_ref[0:keys, :], jnp.ones((keys, LANES), BF16)], axis=1)
        return _dot(e.astype(BF16), v_one)

    n_blk = seq // Q_TILE
    order = [i for pair in zip(reversed(range(n_blk)), range(n_blk)) for i in pair][:n_blk]
    items = [(i, q_ref) for i in order for q_ref in (q0_ref, q1_ref)]
    s_next = scores(*items[0])
    pending = None
    for n, (i, _) in enumerate(items):
        s_cur = s_next
        if n + 1 < len(items):
            s_next = scores(*items[n + 1])
        acc = weighted_values(i, s_cur)
        if pending is None:
            pending = acc
            continue
        a0, a1 = pending, acc
        pending = None
        d0 = a0[:, HEAD_WIDTH:HEAD_WIDTH + 1]
        d1 = a1[:, HEAD_WIDTH:HEAD_WIDTH + 1]
        o = a0[:, :HEAD_WIDTH] * (1.0 / d0) - a1[:, :HEAD_WIDTH] * (lam / d1)
        o = o * lax.rsqrt(jnp.mean(o * o, axis=-1, keepdims=True) + LN_EPS) * gain
        o_ref[i * Q_TILE:(i + 1) * Q_TILE, :] = o.astype(BF16)


def _diff_attn(q0, q1, k, v, lam_params, gain, lambda_init):
    bsz, seq, width = k.shape
    heads = width // HEAD_WIDTH
    q_aug, k_aug = _alibi_tables(heads, seq)
    blk = pl.BlockSpec((None, seq, HEAD_WIDTH), lambda h, b: (b, 0, h))
    return pl.pallas_call(
        functools.partial(_attn_kernel, seq=seq, lambda_init=lambda_init),
        grid=(heads, bsz),
        in_specs=[_const_spec(lam_params.shape), _const_spec(gain.shape),
                  pl.BlockSpec((None, seq, LANES), lambda h, b: (h, 0, 0)),
                  _const_spec(k_aug.shape), blk, blk, blk, blk],
        out_specs=blk,
        out_shape=jax.ShapeDtypeStruct((bsz, seq, width), BF16),
        compiler_params=pltpu.CompilerParams(
            dimension_semantics=("arbitrary", "arbitrary"), vmem_limit_bytes=VMEM_LIMIT),
        name="diff_attn",
    )(lam_params, gain, q_aug, k_aug, q0, q1, k, v)


def _mix_kernel(x_ref, u_ref, uh_ref, attn_ref, wg_ref, bg_ref, wa_ref, cw_ref, cb_ref,
                cg_ref, cbeta_ref, wc_ref, bc_ref, wo_ref, g1_ref, b1_ref, o_ref,
                ext_ref, xb_ref, gates_ref, attnp_ref, conv_ref, *, alpha, conv_width):
    j = pl.program_id(1)
    tm, d = x_ref.shape
    n_slab = ext_ref.shape[0]
    g_cols = wg_ref.shape[1] // n_slab
    a_cols = wa_ref.shape[1] // n_slab
    base = HALO - (conv_width - 1)
    xb_ref[...] = x_ref[...].astype(BF16)

    for s in range(n_slab):
        gsl = slice(s * g_cols, (s + 1) * g_cols)
        g = _dot(xb_ref[...], wg_ref[:, gsl])
        gates_ref[s] = g
        anchor = _zero_after(g[tm - 1:tm, g_cols - LANES:])
        lanes = slice(s * LANES, (s + 1) * LANES)
        halo = uh_ref[:, lanes].astype(F32)
        ext_ref[s, 0:HALO, :] = jnp.where(j > 0, halo, jnp.zeros_like(halo)) + anchor
        ext_ref[s, HALO:, :] = u_ref[:, lanes].astype(F32) + anchor
        for r0 in range(0, tm, CONV_ROWS):
            acc = ext_ref[s, base + r0:base + r0 + CONV_ROWS, :] * cw_ref[0:1, lanes] + cb_ref[:, lanes]
            for k in range(1, conv_width):
                acc = acc + (ext_ref[s, base + k + r0:base + k + r0 + CONV_ROWS, :]
                             * cw_ref[k:k + 1, lanes])
            conv_ref[s, r0:r0 + CONV_ROWS, :] = acc
    for s in range(n_slab):
        asl = slice(s * a_cols, (s + 1) * a_cols)
        attnp_ref[s] = _dot(attn_ref[...], wa_ref[:, asl])

    cat = lambda ref, lo, hi: jnp.concatenate([ref[s] for s in range(lo, hi)], axis=1)
    z = _layer_norm(cat(conv_ref, 0, n_slab), cg_ref[...], cbeta_ref[...])
    z = z * _sigmoid(z)
    conv = _dot(z.astype(BF16), wc_ref[...]) + bc_ref[...]
    half = n_slab // 2
    g_attn = _sigmoid(cat(gates_ref, 0, half) + bg_ref[:, 0:d])
    g_conv = _sigmoid(cat(gates_ref, half, n_slab) + bg_ref[:, d:])
    mixed = g_attn * cat(attnp_ref, 0, n_slab) + g_conv * conv
    y = _dot(mixed.astype(BF16), wo_ref[...])
    o_ref[...] = _layer_norm(alpha * x_ref[...] + y, g1_ref[...], b1_ref[...])


def _mix(x, u, attn, wg, bg, wa, cw, cb, cg, cbeta, wc, bc, wo, g1, b1, alpha):
    bsz, seq, d = x.shape
    ch = u.shape[-1]
    tm = TOKEN_TILE
    halo_blocks = tm // HALO
    n_slab = ch // LANES
    consts = (wg, bg, wa, cw, cb, cg, cbeta, wc, bc, wo, g1, b1)
    return pl.pallas_call(
        functools.partial(_mix_kernel, alpha=alpha, conv_width=cw.shape[0]),
        grid=(bsz, seq // tm),
        in_specs=[
            pl.BlockSpec((None, tm, d), lambda b, j: (b, j, 0)),
            pl.BlockSpec((None, tm, ch), lambda b, j: (b, j, 0)),
            pl.BlockSpec((None, HALO, ch),
                         lambda b, j: (b, jnp.maximum(j * halo_blocks - 1, 0), 0)),
            pl.BlockSpec((None, tm, attn.shape[-1]), lambda b, j: (b, j, 0)),
        ] + [_const_spec(c.shape) for c in consts],
        out_specs=pl.BlockSpec((None, tm, d), lambda b, j: (b, j, 0)),
        out_shape=jax.ShapeDtypeStruct((bsz, seq, d), F32),
        scratch_shapes=[
            pltpu.VMEM((n_slab, tm + HALO, LANES), F32),
            pltpu.VMEM((tm, d), BF16),
            pltpu.VMEM((n_slab, tm, wg.shape[1] // n_slab), F32),
            pltpu.VMEM((n_slab, tm, wa.shape[1] // n_slab), F32),
            pltpu.VMEM((n_slab, tm, LANES), F32),
        ],
        compiler_params=pltpu.CompilerParams(
            dimension_semantics=("arbitrary", "arbitrary"), vmem_limit_bytes=VMEM_LIMIT),
        name="mix",
    )(x, u, u, attn, *consts)


def _ffn_kernel(x_ref, wgate_ref, wval_ref, dw_ref, db_ref, wdown_ref, g2_ref, b2_ref,
                o_ref, ext_ref, carry_ref, hid_ref, *, alpha, conv_width, f_tile):
    j = pl.program_id(1)
    tm, _ = x_ref.shape
    f_dim = hid_ref.shape[-1]

    @pl.when(j == 0)
    def _():
        carry_ref[...] = jnp.zeros_like(carry_ref)

    x = x_ref[...]
    xb = x.astype(BF16)
    base = CARRY - (conv_width - 1)
    for c0 in range(0, f_dim, f_tile):
        cols = slice(c0, c0 + f_tile)
        gate = _dot(xb, wgate_ref[:, cols])
        slabs = []
        for s in range(ext_ref.shape[0]):
            lanes = slice(c0 + s * LANES, c0 + (s + 1) * LANES)
            g_s = gate[:, s * LANES:(s + 1) * LANES]
            ext_ref[s, 0:CARRY, :] = carry_ref[:, lanes]
            ext_ref[s, CARRY:, :] = g_s
            carry_ref[:, lanes] = g_s[tm - CARRY:, :]
            a_s = ext_ref[s, base:base + tm, :] * dw_ref[0:1, lanes] + db_ref[:, lanes]
            for t in range(1, conv_width):
                a_s = a_s + ext_ref[s, base + t:base + t + tm, :] * dw_ref[t:t + 1, lanes]
            slabs.append(a_s)
        acc = jnp.concatenate(slabs, axis=1)
        act = 0.5 * acc * (1.0 + lax.erf(acc * math.sqrt(0.5)))
        val = _dot(xb, wval_ref[:, cols])
        hid_ref[:, cols] = (act * val).astype(BF16)
    f = _dot(hid_ref[...], wdown_ref[...])
    o_ref[...] = _layer_norm(alpha * x + f, g2_ref[...], b2_ref[...])


def _ffn(x, wgate, wval, dw, db, wdown, g2, b2, alpha):
    bsz, seq, d = x.shape
    f_dim = wgate.shape[-1]
    tm = WIDE_TILE
    f_tile = 2 * LANES
    consts = (wgate, wval, dw, db, wdown, g2, b2)
    return pl.pallas_call(
        functools.partial(_ffn_kernel, alpha=alpha, conv_width=dw.shape[0], f_tile=f_tile),
        grid=(bsz, seq // tm),
        in_specs=[pl.BlockSpec((None, tm, d), lambda b, j: (b, j, 0))]
        + [_const_spec(c.shape) for c in consts],
        out_specs=pl.BlockSpec((None, tm, d), lambda b, j: (b, j, 0)),
        out_shape=jax.ShapeDtypeStruct((bsz, seq, d), F32),
        scratch_shapes=[
            pltpu.VMEM((f_tile // LANES, tm + CARRY, LANES), F32),
            pltpu.VMEM((CARRY, f_dim), F32),
            pltpu.VMEM((tm, f_dim), BF16),
        ],
        compiler_params=pltpu.CompilerParams(
            dimension_semantics=("arbitrary", "arbitrary"), vmem_limit_bytes=VMEM_LIMIT),
        name="ffn",
    )(x, *consts)


def kernel(x, w_in, b_gate, lambda_q1, lambda_k1, lambda_q2, lambda_k2, subln_gain, w_attn_proj, conv_dw_w, conv_dw_b, conv_ln_g, conv_ln_b, w_conv_proj, b_conv_proj, w_out, ln1_g, ln1_b, w_ffn_in, ffn_dw_w, ffn_dw_b, w_ffn_down, ln2_g, ln2_b):
    bsz, seq, d = x.shape
    depth = w_in.shape[0]
    width = w_attn_proj.shape[1]
    ch = w_conv_proj.shape[1]
    f_dim = w_ffn_down.shape[1]
    assert width == ATTN_HEADS * HEAD_WIDTH and ch == width
    assert seq % WIDE_TILE == 0 and seq % TOKEN_TILE == 0 and seq % Q_TILE == 0
    assert f_dim % (2 * LANES) == 0
    alpha = (2.0 * depth) ** 0.25
    row = lambda a: a.reshape(1, -1)

    for l in range(depth):
        lambda_init = 0.8 - 0.6 * math.exp(-0.3 * l)
        w_l = w_in[l].astype(BF16)
        n_proj = 3 * width + 2 * ch
        q0, q1, k, v, u = _in_proj(x.reshape(bsz * seq, d), w_l[:, :n_proj], width)
        shape3 = (bsz, seq, width)
        lam_params = jnp.stack([lambda_q1[l], lambda_k1[l], lambda_q2[l], lambda_k2[l]])
        attn = _diff_attn(q0.reshape(shape3), q1.reshape(shape3), k.reshape(shape3),
                          v.reshape(shape3), lam_params, row(subln_gain[l]), lambda_init)
        x = _mix(x, u.reshape(bsz, seq, ch), attn,
                 w_l[:, n_proj:], row(b_gate[l]), w_attn_proj[l].astype(BF16),
                 conv_dw_w[l], row(conv_dw_b[l]), row(conv_ln_g[l]), row(conv_ln_b[l]),
                 w_conv_proj[l].astype(BF16), row(b_conv_proj[l]), w_out[l].astype(BF16),
                 row(ln1_g[l]), row(ln1_b[l]), alpha)
        w_up = w_ffn_in[l].astype(BF16)
        x = _ffn(x, w_up[:, :f_dim], w_up[:, f_dim:], ffn_dw_w[l], row(ffn_dw_b[l]),
                 w_ffn_down[l].astype(BF16), row(ln2_g[l]), row(ln2_b[l]), alpha)
    return x
```

```python
import functools
import math

import jax
import jax.numpy as jnp
from jax import lax
from jax.experimental import pallas as pl
from jax.experimental.pallas import tpu as pltpu

CHUNK = 64
ATTN_HEADS = 4
HEAD_DIM = 64
HEAD_WIDTH = 2 * HEAD_DIM
LN_EPS = 1e-5

LANES = 128
Q_TILE = 256
TOKEN_TILE = 1024
HALO = 32
CARRY = 8
CONV_ROWS = 128
VMEM_LIMIT = 56 * 1024 * 1024

BF16 = jnp.bfloat16
F32 = jnp.float32


def _dot(a, b):
    return jnp.dot(a, b, preferred_element_type=F32)


def _dot_nt(a, b):
    return lax.dot_general(a, b, (((1,), (1,)), ((), ())), preferred_element_type=F32)


def _sigmoid(x):
    return 0.5 + 0.5 * jnp.tanh(0.5 * x)


def _layer_norm(x, g, b):
    mu = jnp.mean(x, axis=-1, keepdims=True)
    xc = x - mu
    var = jnp.mean(xc * xc, axis=-1, keepdims=True)
    return xc * lax.rsqrt(var + LN_EPS) * g + b


def _const_spec(shape):
    nd = len(shape)
    return pl.BlockSpec(shape, lambda *_: (0,) * nd, pipeline_mode=pl.Buffered(1))


def _in_proj_kernel(x_ref, w_ref, q0_ref, q1_ref, k_ref, v_ref, *, width, scale):
    xb = x_ref[...].astype(BF16)
    q = (_dot(xb, w_ref[:, 0:width]) * scale).astype(BF16)
    lane = lax.broadcasted_iota(jnp.int32, q.shape, 1)
    first = (lane & (HEAD_WIDTH - 1)) < HEAD_DIM
    zero = jnp.zeros_like(q)
    q0_ref[...] = jnp.where(first, q, zero)
    q1_ref[...] = jnp.where(first, zero, q)
    k_ref[...] = _dot(xb, w_ref[:, width:2 * width]).astype(BF16)
    v_ref[...] = _dot(xb, w_ref[:, 2 * width:3 * width]).astype(BF16)


def _in_proj(x2d, w, width):
    m, d = x2d.shape
    tm = TOKEN_TILE
    out = jax.ShapeDtypeStruct((m, width), BF16)
    row_spec = pl.BlockSpec((tm, width), lambda i: (i, 0))
    return pl.pallas_call(
        functools.partial(_in_proj_kernel, width=width, scale=HEAD_DIM ** -0.5),
        grid=(m // tm,),
        in_specs=[pl.BlockSpec((tm, d), lambda i: (i, 0)), _const_spec(w.shape)],
        out_specs=[row_spec] * 4,
        out_shape=[out] * 4,
        compiler_params=pltpu.CompilerParams(
            dimension_semantics=("arbitrary",), vmem_limit_bytes=VMEM_LIMIT),
        name="in_proj",
    )(x2d, w)


def _head_slope(h):
    return jnp.exp2(jnp.full((1, 1), -8.0 / ATTN_HEADS, F32) * (h + 1).astype(F32))


def _alibi_kernel(qaug_ref, kaug_ref):
    slope = _head_slope(pl.program_id(0))
    row = lax.broadcasted_iota(jnp.int32, qaug_ref.shape, 0)
    lane = lax.broadcasted_iota(jnp.int32, qaug_ref.shape, 1)
    hi = (row & ~255).astype(F32)
    lo = (row & 255).astype(F32)
    tail = jnp.where(jnp.logical_and(lane >= 2, lane < 4), slope, 0.0)
    qaug_ref[...] = jnp.where(lane == 0, -hi * slope,
                              jnp.where(lane == 1, -lo * slope, tail)).astype(BF16)
    one = jnp.where(lane < 2, 1.0, 0.0)
    kaug_ref[...] = jnp.where(lane == 2, hi, jnp.where(lane == 3, lo, one)).astype(BF16)


def _alibi_tables(heads, seq):
    return pl.pallas_call(
        _alibi_kernel,
        grid=(heads,),
        out_specs=[pl.BlockSpec((None, seq, LANES), lambda h: (h, 0, 0)),
                   pl.BlockSpec((seq, LANES), lambda h: (0, 0))],
        out_shape=[jax.ShapeDtypeStruct((heads, seq, LANES), BF16),
                   jax.ShapeDtypeStruct((seq, LANES), BF16)],
        compiler_params=pltpu.CompilerParams(dimension_semantics=("arbitrary",)),
        name="alibi_tables",
    )()


def _attn_kernel(lam_ref, gain_ref, qaug_ref, kaug_ref, q0_ref, q1_ref, k_ref, v_ref, o_ref,
                 *, seq, lambda_init):
    slope = _head_slope(pl.program_id(0))

    lp = lam_ref[...]
    dot1 = jnp.sum(lp[0:1, :] * lp[1:2, :], axis=-1, keepdims=True)
    dot2 = jnp.sum(lp[2:3, :] * lp[3:4, :], axis=-1, keepdims=True)
    lam = jnp.exp(dot1) - jnp.exp(dot2) + lambda_init
    gain = gain_ref[...] * (1.0 - lambda_init)

    r = lax.broadcasted_iota(jnp.int32, (Q_TILE, Q_TILE), 0)
    c = lax.broadcasted_iota(jnp.int32, (Q_TILE, Q_TILE), 1)
    ahead = jnp.maximum(c - r, 0).astype(F32)
    diag_fix = jnp.where((c // CHUNK) <= (r // CHUNK), -2.0 * slope * ahead, -jnp.inf)

    def scores(i, q_ref):
        rows, keys = slice(i * Q_TILE, (i + 1) * Q_TILE), slice(0, (i + 1) * Q_TILE)
        lhs = jnp.concatenate([q_ref[rows, :], qaug_ref[rows, :]], axis=1)
        rhs = jnp.concatenate([k_ref[keys, :], kaug_ref[keys, :]], axis=1)
        s = _dot_nt(lhs, rhs)
        s_d = s[:, i * Q_TILE:] + diag_fix
        return s_d if i == 0 else jnp.concatenate([s[:, :i * Q_TILE], s_d], axis=1)

    def weighted_values(i, s):
        e = jnp.exp(s - jnp.max(s, axis=-1, keepdims=True))
        keys = (i + 1) * Q_TILE
        v_one = jnp.concatenate([v_ref[0:keys, :], jnp.ones((keys, LANES), BF16)], axis=1)
        return _dot(e.astype(BF16), v_one)

    n_blk = seq // Q_TILE
    order = [i for pair in zip(reversed(range(n_blk)), range(n_blk)) for i in pair][:n_blk]
    items = [(i, q_ref) for i in order for q_ref in (q0_ref, q1_ref)]
    s_next = scores(*items[0])
    pending = None
    for n, (i, _) in enumerate(items):
        s_cur = s_next
        if n + 1 < len(items):
            s_next = scores(*items[n + 1])
        acc = weighted_values(i, s_cur)
        if pending is None:
            pending = acc
            continue
        a0, a1 = pending, acc
        pending = None
        d0 = a0[:, HEAD_WIDTH:HEAD_WIDTH + 1]
        d1 = a1[:, HEAD_WIDTH:HEAD_WIDTH + 1]
        o = a0[:, :HEAD_WIDTH] * (1.0 / d0) - a1[:, :HEAD_WIDTH] * (lam / d1)
        o = o * lax.rsqrt(jnp.mean(o * o, axis=-1, keepdims=True) + LN_EPS) * gain
        o_ref[i * Q_TILE:(i + 1) * Q_TILE, :] = o.astype(BF16)


def _diff_attn(q0, q1, k, v, lam_params, gain, lambda_init):
    bsz, seq, width = k.shape
    heads = width // HEAD_WIDTH
    q_aug, k_aug = _alibi_tables(heads, seq)
    blk = pl.BlockSpec((None, seq, HEAD_WIDTH), lambda h, b: (b, 0, h))
    return pl.pallas_call(
        functools.partial(_attn_kernel, seq=seq, lambda_init=lambda_init),
        grid=(heads, bsz),
        in_specs=[_const_spec(lam_params.shape), _const_spec(gain.shape),
                  pl.BlockSpec((None, seq, LANES), lambda h, b: (h, 0, 0)),
                  _const_spec(k_aug.shape), blk, blk, blk, blk],
        out_specs=blk,
        out_shape=jax.ShapeDtypeStruct((bsz, seq, width), BF16),
        compiler_params=pltpu.CompilerParams(
            dimension_semantics=("arbitrary", "arbitrary"), vmem_limit_bytes=VMEM_LIMIT),
        name="diff_attn",
    )(lam_params, gain, q_aug, k_aug, q0, q1, k, v)


def _mix_kernel(x_ref, attn_ref, wglu_ref, wg_ref, bg_ref, wa_ref, cw_ref, cb_ref,
                cg_ref, cbeta_ref, wc_ref, bc_ref, wo_ref, g1_ref, b1_ref, o_ref,
                ext_ref, gates_ref, attnp_ref, *, alpha, conv_width):
    j = pl.program_id(1)
    tm, d = x_ref.shape
    n_slab = ext_ref.shape[0]
    g_cols = wg_ref.shape[1] // n_slab
    a_cols = wa_ref.shape[1] // n_slab
    base = HALO - (conv_width - 1)

    @pl.when(j == 0)
    def _():
        ext_ref[:, tm:, :] = jnp.zeros((n_slab, HALO, LANES), F32)

    xb = x_ref[...].astype(BF16)
    slabs = []
    for s in range(n_slab):
        pair = _dot(xb, wglu_ref[:, 2 * s * LANES:2 * (s + 1) * LANES])
        ext_ref[s, 0:HALO, :] = ext_ref[s, tm:, :]
        ext_ref[s, HALO:, :] = pair[:, 0:LANES] * _sigmoid(pair[:, LANES:])
        lanes = slice(s * LANES, (s + 1) * LANES)
        blocks = []
        for r0 in range(0, tm, CONV_ROWS):
            acc = ext_ref[s, base + r0:base + r0 + CONV_ROWS, :] * cw_ref[0:1, lanes] + cb_ref[:, lanes]
            for k in range(1, conv_width):
                acc = acc + (ext_ref[s, base + k + r0:base + k + r0 + CONV_ROWS, :]
                             * cw_ref[k:k + 1, lanes])
            blocks.append(acc)
        slabs.append(jnp.concatenate(blocks, axis=0))
        gsl = slice(s * g_cols, (s + 1) * g_cols)
        gates_ref[s] = _sigmoid(_dot(xb, wg_ref[:, gsl]) + bg_ref[:, gsl])
        asl = slice(s * a_cols, (s + 1) * a_cols)
        attnp_ref[s] = _dot(attn_ref[...], wa_ref[:, asl])

    cat = lambda ref, lo, hi: jnp.concatenate([ref[s] for s in range(lo, hi)], axis=1)
    z = _layer_norm(jnp.concatenate(slabs, axis=1), cg_ref[...], cbeta_ref[...])
    z = z * _sigmoid(z)
    conv = _dot(z.astype(BF16), wc_ref[...]) + bc_ref[...]
    half = n_slab // 2
    mixed = cat(gates_ref, 0, half) * cat(attnp_ref, 0, n_slab) + cat(gates_ref, half, n_slab) * conv
    y = _dot(mixed.astype(BF16), wo_ref[...])
    o_ref[...] = _layer_norm(alpha * x_ref[...] + y, g1_ref[...], b1_ref[...])


def _mix(x, attn, wglu, wg, bg, wa, cw, cb, cg, cbeta, wc, bc, wo, g1, b1, alpha):
    bsz, seq, d = x.shape
    ch = cw.shape[1]
    tm = TOKEN_TILE
    n_slab = ch // LANES
    consts = (wglu, wg, bg, wa, cw, cb, cg, cbeta, wc, bc, wo, g1, b1)
    tile = lambda width: pl.BlockSpec((None, tm, width), lambda b, j: (b, j, 0))
    return pl.pallas_call(
        functools.partial(_mix_kernel, alpha=alpha, conv_width=cw.shape[0]),
        grid=(bsz, seq // tm),
        in_specs=[tile(d), tile(attn.shape[-1])] + [_const_spec(c.shape) for c in consts],
        out_specs=tile(d),
        out_shape=jax.ShapeDtypeStruct((bsz, seq, d), F32),
        scratch_shapes=[
            pltpu.VMEM((n_slab, tm + HALO, LANES), F32),
            pltpu.VMEM((n_slab, tm, wg.shape[1] // n_slab), F32),
            pltpu.VMEM((n_slab, tm, wa.shape[1] // n_slab), F32),
        ],
        compiler_params=pltpu.CompilerParams(
            dimension_semantics=("arbitrary", "arbitrary"), vmem_limit_bytes=VMEM_LIMIT),
        name="mix",
    )(x, attn, *consts)


def _ffn_kernel(x_ref, wgate_ref, wval_ref, dw_ref, db_ref, wdown_ref, g2_ref, b2_ref,
                o_ref, ext_ref, carry_ref, hid_ref, *, alpha, conv_width, f_tile):
    j = pl.program_id(1)
    tm, _ = x_ref.shape
    f_dim = hid_ref.shape[-1]

    @pl.when(j == 0)
    def _():
        carry_ref[...] = jnp.zeros_like(carry_ref)

    x = x_ref[...]
    xb = x.astype(BF16)
    base = CARRY - (conv_width - 1)
    for c0 in range(0, f_dim, f_tile):
        cols = slice(c0, c0 + f_tile)
        gate = _dot(xb, wgate_ref[:, cols])
        slabs = []
        for s in range(ext_ref.shape[0]):
            lanes = slice(c0 + s * LANES, c0 + (s + 1) * LANES)
            g_s = gate[:, s * LANES:(s + 1) * LANES]
            ext_ref[s, 0:CARRY, :] = carry_ref[:, lanes]
            ext_ref[s, CARRY:, :] = g_s
            carry_ref[:, lanes] = g_s[tm - CARRY:, :]
            a_s = ext_ref[s, base:base + tm, :] * dw_ref[0:1, lanes] + db_ref[:, lanes]
            for t in range(1, conv_width):
                a_s = a_s + ext_ref[s, base + t:base + t + tm, :] * dw_ref[t:t + 1, lanes]
            slabs.append(a_s)
        acc = jnp.concatenate(slabs, axis=1)
        act = 0.5 * acc * (1.0 + lax.erf(acc * math.sqrt(0.5)))
        val = _dot(xb, wval_ref[:, cols])
        hid_ref[:, cols] = (act * val).astype(BF16)
    f = _dot(hid_ref[...], wdown_ref[...])
    o_ref[...] = _layer_norm(alpha * x + f, g2_ref[...], b2_ref[...])


def _ffn(x, wgate, wval, dw, db, wdown, g2, b2, alpha):
    bsz, seq, d = x.shape
    f_dim = wgate.shape[-1]
    tm = TOKEN_TILE
    f_tile = 2 * LANES
    consts = (wgate, wval, dw, db, wdown, g2, b2)
    return pl.pallas_call(
        functools.partial(_ffn_kernel, alpha=alpha, conv_width=dw.shape[0], f_tile=f_tile),
        grid=(bsz, seq // tm),
        in_specs=[pl.BlockSpec((None, tm, d), lambda b, j: (b, j, 0))]
        + [_const_spec(c.shape) for c in consts],
        out_specs=pl.BlockSpec((None, tm, d), lambda b, j: (b, j, 0)),
        out_shape=jax.ShapeDtypeStruct((bsz, seq, d), F32),
        scratch_shapes=[
            pltpu.VMEM((f_tile // LANES, tm + CARRY, LANES), F32),
            pltpu.VMEM((CARRY, f_dim), F32),
            pltpu.VMEM((tm, f_dim), BF16),
        ],
        compiler_params=pltpu.CompilerParams(
            dimension_semantics=("arbitrary", "arbitrary"), vmem_limit_bytes=VMEM_LIMIT),
        name="ffn",
    )(x, *consts)


def kernel(x, w_in, b_gate, lambda_q1, lambda_k1, lambda_q2, lambda_k2, subln_gain, w_attn_proj, conv_dw_w, conv_dw_b, conv_ln_g, conv_ln_b, w_conv_proj, b_conv_proj, w_out, ln1_g, ln1_b, w_ffn_in, ffn_dw_w, ffn_dw_b, w_ffn_down, ln2_g, ln2_b):
    bsz, seq, d = x.shape
    depth = w_in.shape[0]
    width = w_attn_proj.shape[1]
    ch = w_conv_proj.shape[1]
    f_dim = w_ffn_down.shape[1]
    assert width == ATTN_HEADS * HEAD_WIDTH and ch == width
    assert seq % TOKEN_TILE == 0 and seq % Q_TILE == 0
    assert f_dim % (2 * LANES) == 0
    alpha = (2.0 * depth) ** 0.25
    row = lambda a: a.reshape(1, -1)

    for l in range(depth):
        lambda_init = 0.8 - 0.6 * math.exp(-0.3 * l)
        w_l = w_in[l].astype(BF16)
        n_proj = 3 * width + 2 * ch
        q0, q1, k, v = _in_proj(x.reshape(bsz * seq, d), w_l[:, :3 * width], width)
        w_glu = w_l[:, 3 * width:n_proj].reshape(d, 2, ch // LANES, LANES)
        w_glu = w_glu.transpose(0, 2, 1, 3).reshape(d, 2 * ch)
        shape3 = (bsz, seq, width)
        lam_params = jnp.stack([lambda_q1[l], lambda_k1[l], lambda_q2[l], lambda_k2[l]])
        attn = _diff_attn(q0.reshape(shape3), q1.reshape(shape3), k.reshape(shape3),
                          v.reshape(shape3), lam_params, row(subln_gain[l]), lambda_init)
        x = _mix(x, attn, w_glu,
                 w_l[:, n_proj:], row(b_gate[l]), w_attn_proj[l].astype(BF16),
                 conv_dw_w[l], row(conv_dw_b[l]), row(conv_ln_g[l]), row(conv_ln_b[l]),
                 w_conv_proj[l].astype(BF16), row(b_conv_proj[l]), w_out[l].astype(BF16),
                 row(ln1_g[l]), row(ln1_b[l]), alpha)
        w_up = w_ffn_in[l].astype(BF16)
        x = _ffn(x, w_up[:, :f_dim], w_up[:, f_dim:], ffn_dw_w[l], row(ffn_dw_b[l]),
                 w_ffn_down[l].astype(BF16), row(ln2_g[l]), row(ln2_b[l]), alpha)
    return x
```

```python
import functools
import math

import jax
import jax.numpy as jnp
from jax import lax
from jax.experimental import pallas as pl
from jax.experimental.pallas import tpu as pltpu

CHUNK = 64
ATTN_HEADS = 4
HEAD_DIM = 64
HEAD_WIDTH = 2 * HEAD_DIM
LN_EPS = 1e-5

LANES = 128
Q_TILE = 256
TOKEN_TILE = 1024
HALO = 32
CARRY = 8
CONV_ROWS = 128
ROW_CHUNK = 256
VMEM_LIMIT = 56 * 1024 * 1024

BF16 = jnp.bfloat16
F32 = jnp.float32


def _dot(a, b):
    return jnp.dot(a, b, preferred_element_type=F32)


def _dot_nt(a, b):
    return lax.dot_general(a, b, (((1,), (1,)), ((), ())), preferred_element_type=F32)


def _sigmoid(x):
    return 0.5 + 0.5 * jnp.tanh(0.5 * x)


def _layer_norm(x, g, b):
    mu = jnp.mean(x, axis=-1, keepdims=True)
    xc = x - mu
    var = jnp.mean(xc * xc, axis=-1, keepdims=True)
    return xc * lax.rsqrt(var + LN_EPS) * g + b


def _const_spec(shape):
    nd = len(shape)
    return pl.BlockSpec(shape, lambda *_: (0,) * nd, pipeline_mode=pl.Buffered(1))


def _start_sequence(ext_ref, tm):
    ext_ref[:, tm:, :] = jnp.zeros((ext_ref.shape[0], HALO, LANES), F32)


def _glu_conv_slab(xb, wglu_ref, cw_ref, cb_ref, ext_ref, s, conv_width):
    tm = xb.shape[0]
    base = HALO - (conv_width - 1)
    pair = _dot(xb, wglu_ref[:, 2 * s * LANES:2 * (s + 1) * LANES])
    ext_ref[s, 0:HALO, :] = ext_ref[s, tm:, :]
    ext_ref[s, HALO:, :] = pair[:, 0:LANES] * _sigmoid(pair[:, LANES:])
    lanes = slice(s * LANES, (s + 1) * LANES)
    blocks = []
    for r0 in range(0, tm, CONV_ROWS):
        acc = ext_ref[s, base + r0:base + r0 + CONV_ROWS, :] * cw_ref[0:1, lanes] + cb_ref[:, lanes]
        for k in range(1, conv_width):
            acc = acc + (ext_ref[s, base + k + r0:base + k + r0 + CONV_ROWS, :]
                         * cw_ref[k:k + 1, lanes])
        blocks.append(acc)
    return jnp.concatenate(blocks, axis=0)


def _in_proj_kernel(x_ref, w_ref, q0_ref, q1_ref, k_ref, v_ref, *, width, scale):
    xb = x_ref[...].astype(BF16)
    q = (_dot(xb, w_ref[:, 0:width]) * scale).astype(BF16)
    lane = lax.broadcasted_iota(jnp.int32, q.shape, 1)
    first = (lane & (HEAD_WIDTH - 1)) < HEAD_DIM
    zero = jnp.zeros_like(q)
    q0_ref[...] = jnp.where(first, q, zero)
    q1_ref[...] = jnp.where(first, zero, q)
    k_ref[...] = _dot(xb, w_ref[:, width:2 * width]).astype(BF16)
    v_ref[...] = _dot(xb, w_ref[:, 2 * width:3 * width]).astype(BF16)


def _in_proj(x2d, w, width):
    m, d = x2d.shape
    tm = TOKEN_TILE
    out = jax.ShapeDtypeStruct((m, width), BF16)
    row_spec = pl.BlockSpec((tm, width), lambda i: (i, 0))
    return pl.pallas_call(
        functools.partial(_in_proj_kernel, width=width, scale=HEAD_DIM ** -0.5),
        grid=(m // tm,),
        in_specs=[pl.BlockSpec((tm, d), lambda i: (i, 0)), _const_spec(w.shape)],
        out_specs=[row_spec] * 4,
        out_shape=[out] * 4,
        compiler_params=pltpu.CompilerParams(
            dimension_semantics=("arbitrary",), vmem_limit_bytes=VMEM_LIMIT),
        name="in_proj",
    )(x2d, w)


def _head_slope(h):
    return jnp.exp2(jnp.full((1, 1), -8.0 / ATTN_HEADS, F32) * (h + 1).astype(F32))


def _alibi_kernel(qaug_ref, kaug_ref):
    slope = _head_slope(pl.program_id(0))
    row = lax.broadcasted_iota(jnp.int32, qaug_ref.shape, 0)
    lane = lax.broadcasted_iota(jnp.int32, qaug_ref.shape, 1)
    hi = (row & ~255).astype(F32)
    lo = (row & 255).astype(F32)
    tail = jnp.where(jnp.logical_and(lane >= 2, lane < 4), slope, 0.0)
    qaug_ref[...] = jnp.where(lane == 0, -hi * slope,
                              jnp.where(lane == 1, -lo * slope, tail)).astype(BF16)
    one = jnp.where(lane < 2, 1.0, 0.0)
    kaug_ref[...] = jnp.where(lane == 2, hi, jnp.where(lane == 3, lo, one)).astype(BF16)


def _alibi_tables(heads, seq):
    return pl.pallas_call(
        _alibi_kernel,
        grid=(heads,),
        out_specs=[pl.BlockSpec((None, seq, LANES), lambda h: (h, 0, 0)),
                   pl.BlockSpec((seq, LANES), lambda h: (0, 0))],
        out_shape=[jax.ShapeDtypeStruct((heads, seq, LANES), BF16),
                   jax.ShapeDtypeStruct((seq, LANES), BF16)],
        compiler_params=pltpu.CompilerParams(dimension_semantics=("arbitrary",)),
        name="alibi_tables",
    )()


def _attn_kernel(lam_ref, gain_ref, qaug_ref, kaug_ref, q0_ref, q1_ref, k_ref, v_ref, o_ref,
                 *, seq, lambda_init):
    slope = _head_slope(pl.program_id(0))

    lp = lam_ref[...]
    dot1 = jnp.sum(lp[0:1, :] * lp[1:2, :], axis=-1, keepdims=True)
    dot2 = jnp.sum(lp[2:3, :] * lp[3:4, :], axis=-1, keepdims=True)
    lam = jnp.exp(dot1) - jnp.exp(dot2) + lambda_init
    gain = gain_ref[...] * (1.0 - lambda_init)

    r = lax.broadcasted_iota(jnp.int32, (Q_TILE, Q_TILE), 0)
    c = lax.broadcasted_iota(jnp.int32, (Q_TILE, Q_TILE), 1)
    ahead = jnp.maximum(c - r, 0).astype(F32)
    diag_fix = jnp.where((c // CHUNK) <= (r // CHUNK), -2.0 * slope * ahead, -jnp.inf)

    def scores(i, q_ref):
        rows, keys = slice(i * Q_TILE, (i + 1) * Q_TILE), slice(0, (i + 1) * Q_TILE)
        lhs = jnp.concatenate([q_ref[rows, :], qaug_ref[rows, :]], axis=1)
        rhs = jnp.concatenate([k_ref[keys, :], kaug_ref[keys, :]], axis=1)
        s = _dot_nt(lhs, rhs)
        s_d = s[:, i * Q_TILE:] + diag_fix
        return s_d if i == 0 else jnp.concatenate([s[:, :i * Q_TILE], s_d], axis=1)

    def weighted_values(i, s):
        e = jnp.exp(s - jnp.max(s, axis=-1, keepdims=True))
        keys = (i + 1) * Q_TILE
        v_one = jnp.concatenate([v_ref[0:keys, :], jnp.ones((keys, LANES), BF16)], axis=1)
        return _dot(e.astype(BF16), v_one)

    n_blk = seq // Q_TILE
    order = [i for pair in zip(reversed(range(n_blk)), range(n_blk)) for i in pair][:n_blk]
    items = [(i, q_ref) for i in order for q_ref in (q0_ref, q1_ref)]
    s_next = scores(*items[0])
    pending = None
    for n, (i, _) in enumerate(items):
        s_cur = s_next
        if n + 1 < len(items):
            s_next = scores(*items[n + 1])
        acc = weighted_values(i, s_cur)
        if pending is None:
            pending = acc
            continue
        a0, a1 = pending, acc
        pending = None
        d0 = a0[:, HEAD_WIDTH:HEAD_WIDTH + 1]
        d1 = a1[:, HEAD_WIDTH:HEAD_WIDTH + 1]
        o = a0[:, :HEAD_WIDTH] * (1.0 / d0) - a1[:, :HEAD_WIDTH] * (lam / d1)
        o = o * lax.rsqrt(jnp.mean(o * o, axis=-1, keepdims=True) + LN_EPS) * gain
        o_ref[i * Q_TILE:(i + 1) * Q_TILE, :] = o.astype(BF16)


def _diff_attn(q0, q1, k, v, lam_params, gain, lambda_init):
    bsz, seq, width = k.shape
    heads = width // HEAD_WIDTH
    q_aug, k_aug = _alibi_tables(heads, seq)
    blk = pl.BlockSpec((None, seq, HEAD_WIDTH), lambda h, b: (b, 0, h))
    return pl.pallas_call(
        functools.partial(_attn_kernel, seq=seq, lambda_init=lambda_init),
        grid=(heads, bsz),
        in_specs=[_const_spec(lam_params.shape), _const_spec(gain.shape),
                  pl.BlockSpec((None, seq, LANES), lambda h, b: (h, 0, 0)),
                  _const_spec(k_aug.shape), blk, blk, blk, blk],
        out_specs=blk,
        out_shape=jax.ShapeDtypeStruct((bsz, seq, width), BF16),
        compiler_params=pltpu.CompilerParams(
            dimension_semantics=("arbitrary", "arbitrary"), vmem_limit_bytes=VMEM_LIMIT),
        name="diff_attn",
    )(lam_params, gain, q_aug, k_aug, q0, q1, k, v)


def _mix_kernel(x_ref, attn_ref, wglu_ref, wg_ref, bg_ref, wa_ref, cw_ref, cb_ref,
                cg_ref, cbeta_ref, wc_ref, bc_ref, wo_ref, g1_ref, b1_ref, o_ref,
                ext_ref, gates_ref, attnp_ref, *, alpha, conv_width):
    tm, d = x_ref.shape
    n_chunk = gates_ref.shape[0]
    g_cols = wg_ref.shape[1] // n_chunk
    a_cols = wa_ref.shape[1] // n_chunk

    @pl.when(pl.program_id(1) == 0)
    def _():
        _start_sequence(ext_ref, tm)

    xb = x_ref[...].astype(BF16)
    slabs = []
    for s in range(n_chunk):
        slabs.append(_glu_conv_slab(xb, wglu_ref, cw_ref, cb_ref, ext_ref, s, conv_width))
        gsl = slice(s * g_cols, (s + 1) * g_cols)
        gates_ref[s] = _sigmoid(_dot(xb, wg_ref[:, gsl]) + bg_ref[:, gsl])
        asl = slice(s * a_cols, (s + 1) * a_cols)
        attnp_ref[s] = _dot(attn_ref[...], wa_ref[:, asl])

    half = n_chunk // 2
    for r0 in range(0, tm, ROW_CHUNK):
        rows = slice(r0, r0 + ROW_CHUNK)
        cat = lambda ref, lo, hi: jnp.concatenate([ref[s, rows, :] for s in range(lo, hi)], axis=1)
        z = jnp.concatenate([slab[rows, :] for slab in slabs], axis=1)
        z = _layer_norm(z, cg_ref[...], cbeta_ref[...])
        z = z * _sigmoid(z)
        conv = _dot(z.astype(BF16), wc_ref[...]) + bc_ref[...]
        mixed = (cat(gates_ref, 0, half) * cat(attnp_ref, 0, n_chunk)
                 + cat(gates_ref, half, n_chunk) * conv)
        y = _dot(mixed.astype(BF16), wo_ref[...])
        o_ref[rows, :] = _layer_norm(alpha * x_ref[rows, :] + y, g1_ref[...], b1_ref[...])


def _mix(x, attn, wglu, wg, bg, wa, cw, cb, cg, cbeta, wc, bc, wo, g1, b1, alpha):
    bsz, seq, d = x.shape
    tm = TOKEN_TILE
    n_chunk = cw.shape[1] // LANES
    consts = (wglu, wg, bg, wa, cw, cb, cg, cbeta, wc, bc, wo, g1, b1)
    tile = lambda width: pl.BlockSpec((None, tm, width), lambda b, j: (b, j, 0))
    return pl.pallas_call(
        functools.partial(_mix_kernel, alpha=alpha, conv_width=cw.shape[0]),
        grid=(bsz, seq // tm),
        in_specs=[tile(d), tile(attn.shape[-1])] + [_const_spec(c.shape) for c in consts],
        out_specs=tile(d),
        out_shape=jax.ShapeDtypeStruct((bsz, seq, d), F32),
        scratch_shapes=[
            pltpu.VMEM((cw.shape[1] // LANES, tm + HALO, LANES), F32),
            pltpu.VMEM((n_chunk, tm, wg.shape[1] // n_chunk), F32),
            pltpu.VMEM((n_chunk, tm, wa.shape[1] // n_chunk), F32),
        ],
        compiler_params=pltpu.CompilerParams(
            dimension_semantics=("arbitrary", "arbitrary"), vmem_limit_bytes=VMEM_LIMIT),
        name="mix",
    )(x, attn, *consts)


def _ffn_kernel(x_ref, wgate_ref, wval_ref, dw_ref, db_ref, wdown_ref, g2_ref, b2_ref,
                o_ref, ext_ref, carry_ref, hid_ref, *, alpha, conv_width, f_tile):
    j = pl.program_id(1)
    tm, _ = x_ref.shape
    f_dim = hid_ref.shape[-1]

    @pl.when(j == 0)
    def _():
        carry_ref[...] = jnp.zeros_like(carry_ref)

    x = x_ref[...]
    xb = x.astype(BF16)
    base = CARRY - (conv_width - 1)
    for c0 in range(0, f_dim, f_tile):
        cols = slice(c0, c0 + f_tile)
        gate = _dot(xb, wgate_ref[:, cols])
        slabs = []
        for s in range(ext_ref.shape[0]):
            lanes = slice(c0 + s * LANES, c0 + (s + 1) * LANES)
            g_s = gate[:, s * LANES:(s + 1) * LANES]
            ext_ref[s, 0:CARRY, :] = carry_ref[:, lanes]
            ext_ref[s, CARRY:, :] = g_s
            carry_ref[:, lanes] = g_s[tm - CARRY:, :]
            a_s = ext_ref[s, base:base + tm, :] * dw_ref[0:1, lanes] + db_ref[:, lanes]
            for t in range(1, conv_width):
                a_s = a_s + ext_ref[s, base + t:base + t + tm, :] * dw_ref[t:t + 1, lanes]
            slabs.append(a_s)
        acc = jnp.concatenate(slabs, axis=1)
        act = 0.5 * acc * (1.0 + lax.erf(acc * math.sqrt(0.5)))
        val = _dot(xb, wval_ref[:, cols])
        hid_ref[:, cols] = (act * val).astype(BF16)
    for r0 in range(0, tm, ROW_CHUNK):
        rows = slice(r0, r0 + ROW_CHUNK)
        f = _dot(hid_ref[rows, :], wdown_ref[...])
        o_ref[rows, :] = _layer_norm(alpha * x_ref[rows, :] + f, g2_ref[...], b2_ref[...])


def _ffn(x, wgate, wval, dw, db, wdown, g2, b2, alpha):
    bsz, seq, d = x.shape
    f_dim = wgate.shape[-1]
    tm = TOKEN_TILE
    f_tile = 2 * LANES
    consts = (wgate, wval, dw, db, wdown, g2, b2)
    return pl.pallas_call(
        functools.partial(_ffn_kernel, alpha=alpha, conv_width=dw.shape[0], f_tile=f_tile),
        grid=(bsz, seq // tm),
        in_specs=[pl.BlockSpec((None, tm, d), lambda b, j: (b, j, 0))]
        + [_const_spec(c.shape) for c in consts],
        out_specs=pl.BlockSpec((None, tm, d), lambda b, j: (b, j, 0)),
        out_shape=jax.ShapeDtypeStruct((bsz, seq, d), F32),
        scratch_shapes=[
            pltpu.VMEM((f_tile // LANES, tm + CARRY, LANES), F32),
            pltpu.VMEM((CARRY, f_dim), F32),
            pltpu.VMEM((tm, f_dim), BF16),
        ],
        compiler_params=pltpu.CompilerParams(
            dimension_semantics=("arbitrary", "arbitrary"), vmem_limit_bytes=VMEM_LIMIT),
        name="ffn",
    )(x, *consts)


def kernel(x, w_in, b_gate, lambda_q1, lambda_k1, lambda_q2, lambda_k2, subln_gain, w_attn_proj, conv_dw_w, conv_dw_b, conv_ln_g, conv_ln_b, w_conv_proj, b_conv_proj, w_out, ln1_g, ln1_b, w_ffn_in, ffn_dw_w, ffn_dw_b, w_ffn_down, ln2_g, ln2_b):
    bsz, seq, d = x.shape
    depth = w_in.shape[0]
    width = w_attn_proj.shape[1]
    ch = w_conv_proj.shape[1]
    f_dim = w_ffn_down.shape[1]
    assert width == ATTN_HEADS * HEAD_WIDTH and ch == width
    assert seq % TOKEN_TILE == 0 and seq % Q_TILE == 0
    assert f_dim % (2 * LANES) == 0
    alpha = (2.0 * depth) ** 0.25
    row = lambda a: a.reshape(1, -1)

    for l in range(depth):
        lambda_init = 0.8 - 0.6 * math.exp(-0.3 * l)
        n_proj = 3 * width + 2 * ch
        w_qkv = w_in[l][:, :3 * width].astype(BF16)
        w_glu = w_in[l][:, 3 * width:n_proj].reshape(d, 2, ch // LANES, LANES)
        w_glu = w_glu.transpose(0, 2, 1, 3).reshape(d, 2 * ch).astype(BF16)
        w_gate = w_in[l][:, n_proj:].astype(BF16)
        q0, q1, k, v = _in_proj(x.reshape(bsz * seq, d), w_qkv, width)
        shape3 = (bsz, seq, width)
        lam_params = jnp.stack([lambda_q1[l], lambda_k1[l], lambda_q2[l], lambda_k2[l]])
        attn = _diff_attn(q0.reshape(shape3), q1.reshape(shape3), k.reshape(shape3),
                          v.reshape(shape3), lam_params, row(subln_gain[l]), lambda_init)
        x = _mix(x, attn, w_glu,
                 w_gate, row(b_gate[l]), w_attn_proj[l].astype(BF16),
                 conv_dw_w[l], row(conv_dw_b[l]), row(conv_ln_g[l]), row(conv_ln_b[l]),
                 w_conv_proj[l].astype(BF16), row(b_conv_proj[l]), w_out[l].astype(BF16),
                 row(ln1_g[l]), row(ln1_b[l]), alpha)
        x = _ffn(x, w_ffn_in[l][:, :f_dim].astype(BF16), w_ffn_in[l][:, f_dim:].astype(BF16),
                 ffn_dw_w[l], row(ffn_dw_b[l]),
                 w_ffn_down[l].astype(BF16), row(ln2_g[l]), row(ln2_b[l]), alpha)
    return x
```

```python
import functools
import math

import jax
import jax.numpy as jnp
from jax import lax
from jax.experimental import pallas as pl
from jax.experimental.pallas import tpu as pltpu

CHUNK = 64
ATTN_HEADS = 4
HEAD_DIM = 64
HEAD_WIDTH = 2 * HEAD_DIM
LN_EPS = 1e-5

LANES = 128
Q_TILE = 256
ATTN_BATCH = 2
TOKEN_TILE = 1024
HALO = 32
CARRY = 8
CONV_ROWS = 128
ROW_CHUNK = 256
VMEM_LIMIT = 56 * 1024 * 1024

BF16 = jnp.bfloat16
F32 = jnp.float32


def _dot(a, b):
    return jnp.dot(a, b, preferred_element_type=F32)


def _dot_nt(a, b):
    return lax.dot_general(a, b, (((1,), (1,)), ((), ())), preferred_element_type=F32)


def _sigmoid(x):
    return 0.5 + 0.5 * jnp.tanh(0.5 * x)


def _layer_norm(x, g, b):
    mu = jnp.mean(x, axis=-1, keepdims=True)
    xc = x - mu
    var = jnp.mean(xc * xc, axis=-1, keepdims=True)
    return xc * lax.rsqrt(var + LN_EPS) * g + b


def _const_spec(shape):
    nd = len(shape)
    return pl.BlockSpec(shape, lambda *_: (0,) * nd, pipeline_mode=pl.Buffered(1))


def _start_sequence(ext_ref, tm):
    ext_ref[:, tm:, :] = jnp.zeros((ext_ref.shape[0], HALO, LANES), F32)


def _glu_conv_slab(xb, wglu_ref, cw_ref, cb_ref, ext_ref, s, conv_width):
    tm = xb.shape[0]
    base = HALO - (conv_width - 1)
    pair = _dot(xb, wglu_ref[:, 2 * s * LANES:2 * (s + 1) * LANES])
    ext_ref[s, 0:HALO, :] = ext_ref[s, tm:, :]
    ext_ref[s, HALO:, :] = pair[:, 0:LANES] * _sigmoid(pair[:, LANES:])
    lanes = slice(s * LANES, (s + 1) * LANES)
    blocks = []
    for r0 in range(0, tm, CONV_ROWS):
        acc = ext_ref[s, base + r0:base + r0 + CONV_ROWS, :] * cw_ref[0:1, lanes] + cb_ref[:, lanes]
        for k in range(1, conv_width):
            acc = acc + (ext_ref[s, base + k + r0:base + k + r0 + CONV_ROWS, :]
                         * cw_ref[k:k + 1, lanes])
        blocks.append(acc)
    return jnp.concatenate(blocks, axis=0)


def _in_proj_kernel(x_ref, w_ref, q0_ref, q1_ref, k_ref, v_ref, *, width, scale):
    xb = x_ref[...].astype(BF16)
    q = (_dot(xb, w_ref[:, 0:width]) * scale).astype(BF16)
    lane = lax.broadcasted_iota(jnp.int32, q.shape, 1)
    first = (lane & (HEAD_WIDTH - 1)) < HEAD_DIM
    zero = jnp.zeros_like(q)
    q0_ref[...] = jnp.where(first, q, zero)
    q1_ref[...] = jnp.where(first, zero, q)
    k_ref[...] = _dot(xb, w_ref[:, width:2 * width]).astype(BF16)
    v_ref[...] = _dot(xb, w_ref[:, 2 * width:3 * width]).astype(BF16)


def _in_proj(x2d, w, width):
    m, d = x2d.shape
    tm = TOKEN_TILE
    out = jax.ShapeDtypeStruct((m, width), BF16)
    row_spec = pl.BlockSpec((tm, width), lambda i: (i, 0))
    return pl.pallas_call(
        functools.partial(_in_proj_kernel, width=width, scale=HEAD_DIM ** -0.5),
        grid=(m // tm,),
        in_specs=[pl.BlockSpec((tm, d), lambda i: (i, 0)), _const_spec(w.shape)],
        out_specs=[row_spec] * 4,
        out_shape=[out] * 4,
        compiler_params=pltpu.CompilerParams(
            dimension_semantics=("arbitrary",), vmem_limit_bytes=VMEM_LIMIT),
        name="in_proj",
    )(x2d, w)


def _head_slope(h):
    return jnp.exp2(jnp.full((1, 1), -8.0 / ATTN_HEADS, F32) * (h + 1).astype(F32))


def _alibi_kernel(qaug_ref, kaug_ref):
    slope = _head_slope(pl.program_id(0))
    row = lax.broadcasted_iota(jnp.int32, qaug_ref.shape, 0)
    lane = lax.broadcasted_iota(jnp.int32, qaug_ref.shape, 1)
    hi = (row & ~255).astype(F32)
    lo = (row & 255).astype(F32)
    tail = jnp.where(jnp.logical_and(lane >= 2, lane < 4), slope, 0.0)
    qaug_ref[...] = jnp.where(lane == 0, -hi * slope,
                              jnp.where(lane == 1, -lo * slope, tail)).astype(BF16)
    one = jnp.where(lane < 2, 1.0, 0.0)
    kaug_ref[...] = jnp.where(lane == 2, hi, jnp.where(lane == 3, lo, one)).astype(BF16)


def _alibi_tables(heads, seq):
    return pl.pallas_call(
        _alibi_kernel,
        grid=(heads,),
        out_specs=[pl.BlockSpec((None, seq, LANES), lambda h: (h, 0, 0)),
                   pl.BlockSpec((seq, LANES), lambda h: (0, 0))],
        out_shape=[jax.ShapeDtypeStruct((heads, seq, LANES), BF16),
                   jax.ShapeDtypeStruct((seq, LANES), BF16)],
        compiler_params=pltpu.CompilerParams(dimension_semantics=("arbitrary",)),
        name="alibi_tables",
    )()


def _attn_kernel(lam_ref, gain_ref, qaug_ref, kaug_ref, q0_ref, q1_ref, k_ref, v_ref, o_ref,
                 *, seq, lambda_init):
    slope = _head_slope(pl.program_id(0))

    lp = lam_ref[...]
    dot1 = jnp.sum(lp[0:1, :] * lp[1:2, :], axis=-1, keepdims=True)
    dot2 = jnp.sum(lp[2:3, :] * lp[3:4, :], axis=-1, keepdims=True)
    lam = jnp.exp(dot1) - jnp.exp(dot2) + lambda_init
    gain = gain_ref[...] * (1.0 - lambda_init)

    r = lax.broadcasted_iota(jnp.int32, (Q_TILE, Q_TILE), 0)
    c = lax.broadcasted_iota(jnp.int32, (Q_TILE, Q_TILE), 1)
    ahead = jnp.maximum(c - r, 0).astype(F32)
    diag_fix = jnp.where((c // CHUNK) <= (r // CHUNK), -2.0 * slope * ahead, -jnp.inf)

    def scores(b, i, q_ref):
        rows, keys = slice(i * Q_TILE, (i + 1) * Q_TILE), slice(0, (i + 1) * Q_TILE)
        lhs = jnp.concatenate([q_ref[b, rows, :], qaug_ref[rows, :]], axis=1)
        rhs = jnp.concatenate([k_ref[b, keys, :], kaug_ref[keys, :]], axis=1)
        s = _dot_nt(lhs, rhs)
        s_d = s[:, i * Q_TILE:] + diag_fix
        return s_d if i == 0 else jnp.concatenate([s[:, :i * Q_TILE], s_d], axis=1)

    def weighted_values(b, i, s):
        e = jnp.exp(s - jnp.max(s, axis=-1, keepdims=True))
        keys = (i + 1) * Q_TILE
        v_one = jnp.concatenate([v_ref[b, 0:keys, :], jnp.ones((keys, LANES), BF16)], axis=1)
        return _dot(e.astype(BF16), v_one)

    n_blk = seq // Q_TILE
    order = [i for pair in zip(reversed(range(n_blk)), range(n_blk)) for i in pair][:n_blk]
    items = [(b, i, q_ref) for b in range(o_ref.shape[0]) for i in order
             for q_ref in (q0_ref, q1_ref)]
    s_next = scores(*items[0])
    pending = None
    for n, (b, i, _) in enumerate(items):
        s_cur = s_next
        if n + 1 < len(items):
            s_next = scores(*items[n + 1])
        acc = weighted_values(b, i, s_cur)
        if pending is None:
            pending = acc
            continue
        a0, a1 = pending, acc
        pending = None
        d0 = a0[:, HEAD_WIDTH:HEAD_WIDTH + 1]
        d1 = a1[:, HEAD_WIDTH:HEAD_WIDTH + 1]
        o = a0[:, :HEAD_WIDTH] * (1.0 / d0) - a1[:, :HEAD_WIDTH] * (lam / d1)
        o = o * lax.rsqrt(jnp.mean(o * o, axis=-1, keepdims=True) + LN_EPS) * gain
        o_ref[b, i * Q_TILE:(i + 1) * Q_TILE, :] = o.astype(BF16)


def _diff_attn(q0, q1, k, v, lam_params, gain, lambda_init):
    bsz, seq, width = k.shape
    heads = width // HEAD_WIDTH
    q_aug, k_aug = _alibi_tables(heads, seq)
    blk = pl.BlockSpec((ATTN_BATCH, seq, HEAD_WIDTH), lambda h, b: (b, 0, h))
    return pl.pallas_call(
        functools.partial(_attn_kernel, seq=seq, lambda_init=lambda_init),
        grid=(heads, bsz // ATTN_BATCH),
        in_specs=[_const_spec(lam_params.shape), _const_spec(gain.shape),
                  pl.BlockSpec((None, seq, LANES), lambda h, b: (h, 0, 0)),
                  _const_spec(k_aug.shape), blk, blk, blk, blk],
        out_specs=blk,
        out_shape=jax.ShapeDtypeStruct((bsz, seq, width), BF16),
        compiler_params=pltpu.CompilerParams(
            dimension_semantics=("arbitrary", "arbitrary"), vmem_limit_bytes=VMEM_LIMIT),
        name="diff_attn",
    )(lam_params, gain, q_aug, k_aug, q0, q1, k, v)


def _mix_kernel(x_ref, attn_ref, wglu_ref, wg_ref, bg_ref, wa_ref, cw_ref, cb_ref,
                cg_ref, cbeta_ref, wc_ref, bc_ref, wo_ref, g1_ref, b1_ref, o_ref,
                ext_ref, gates_ref, attnp_ref, *, alpha, conv_width):
    tm, d = x_ref.shape
    n_chunk = gates_ref.shape[0]
    g_cols = wg_ref.shape[1] // n_chunk
    a_cols = wa_ref.shape[1] // n_chunk

    @pl.when(pl.program_id(1) == 0)
    def _():
        _start_sequence(ext_ref, tm)

    xb = x_ref[...].astype(BF16)
    slabs = []
    for s in range(n_chunk):
        slabs.append(_glu_conv_slab(xb, wglu_ref, cw_ref, cb_ref, ext_ref, s, conv_width))
        gsl = slice(s * g_cols, (s + 1) * g_cols)
        gates_ref[s] = _sigmoid(_dot(xb, wg_ref[:, gsl]) + bg_ref[:, gsl])
        asl = slice(s * a_cols, (s + 1) * a_cols)
        attnp_ref[s] = _dot(attn_ref[...], wa_ref[:, asl])

    cat = lambda ref, lo, hi: jnp.concatenate([ref[s] for s in range(lo, hi)], axis=1)
    z = _layer_norm(jnp.concatenate(slabs, axis=1), cg_ref[...], cbeta_ref[...])
    z = z * _sigmoid(z)
    conv = _dot(z.astype(BF16), wc_ref[...]) + bc_ref[...]
    half = n_chunk // 2
    mixed = cat(gates_ref, 0, half) * cat(attnp_ref, 0, n_chunk) + cat(gates_ref, half, n_chunk) * conv
    y = _dot(mixed.astype(BF16), wo_ref[...])
    o_ref[...] = _layer_norm(alpha * x_ref[...] + y, g1_ref[...], b1_ref[...])


def _mix(x, attn, wglu, wg, bg, wa, cw, cb, cg, cbeta, wc, bc, wo, g1, b1, alpha):
    bsz, seq, d = x.shape
    tm = TOKEN_TILE
    n_chunk = cw.shape[1] // LANES
    consts = (wglu, wg, bg, wa, cw, cb, cg, cbeta, wc, bc, wo, g1, b1)
    tile = lambda width: pl.BlockSpec((None, tm, width), lambda b, j: (b, j, 0))
    return pl.pallas_call(
        functools.partial(_mix_kernel, alpha=alpha, conv_width=cw.shape[0]),
        grid=(bsz, seq // tm),
        in_specs=[tile(d), tile(attn.shape[-1])] + [_const_spec(c.shape) for c in consts],
        out_specs=tile(d),
        out_shape=jax.ShapeDtypeStruct((bsz, seq, d), F32),
        scratch_shapes=[
            pltpu.VMEM((cw.shape[1] // LANES, tm + HALO, LANES), F32),
            pltpu.VMEM((n_chunk, tm, wg.shape[1] // n_chunk), F32),
            pltpu.VMEM((n_chunk, tm, wa.shape[1] // n_chunk), F32),
        ],
        compiler_params=pltpu.CompilerParams(
            dimension_semantics=("arbitrary", "arbitrary"), vmem_limit_bytes=VMEM_LIMIT),
        name="mix",
    )(x, attn, *consts)


def _ffn_kernel(x_ref, wgate_ref, wval_ref, dw_ref, db_ref, wdown_ref, g2_ref, b2_ref,
                o_ref, ext_ref, carry_ref, hid_ref, *, alpha, conv_width, f_tile):
    j = pl.program_id(1)
    tm, _ = x_ref.shape
    f_dim = hid_ref.shape[-1]

    @pl.when(j == 0)
    def _():
        carry_ref[...] = jnp.zeros_like(carry_ref)

    x = x_ref[...]
    xb = x.astype(BF16)
    base = CARRY - (conv_width - 1)
    for c0 in range(0, f_dim, f_tile):
        cols = slice(c0, c0 + f_tile)
        gate = _dot(xb, wgate_ref[:, cols])
        slabs = []
        for s in range(ext_ref.shape[0]):
            lanes = slice(c0 + s * LANES, c0 + (s + 1) * LANES)
            g_s = gate[:, s * LANES:(s + 1) * LANES]
            ext_ref[s, 0:CARRY, :] = carry_ref[:, lanes]
            ext_ref[s, CARRY:, :] = g_s
            carry_ref[:, lanes] = g_s[tm - CARRY:, :]
            a_s = ext_ref[s, base:base + tm, :] * dw_ref[0:1, lanes] + db_ref[:, lanes]
            for t in range(1, conv_width):
                a_s = a_s + ext_ref[s, base + t:base + t + tm, :] * dw_ref[t:t + 1, lanes]
            slabs.append(a_s)
        acc = jnp.concatenate(slabs, axis=1)
        act = 0.5 * acc * (1.0 + lax.erf(acc * math.sqrt(0.5)))
        val = _dot(xb, wval_ref[:, cols])
        hid_ref[:, cols] = (act * val).astype(BF16)
    for r0 in range(0, tm, ROW_CHUNK):
        rows = slice(r0, r0 + ROW_CHUNK)
        f = _dot(hid_ref[rows, :], wdown_ref[...])
        o_ref[rows, :] = _layer_norm(alpha * x_ref[rows, :] + f, g2_ref[...], b2_ref[...])


def _ffn(x, wgate, wval, dw, db, wdown, g2, b2, alpha):
    bsz, seq, d = x.shape
    f_dim = wgate.shape[-1]
    tm = TOKEN_TILE
    f_tile = 2 * LANES
    consts = (wgate, wval, dw, db, wdown, g2, b2)
    return pl.pallas_call(
        functools.partial(_ffn_kernel, alpha=alpha, conv_width=dw.shape[0], f_tile=f_tile),
        grid=(bsz, seq // tm),
        in_specs=[pl.BlockSpec((None, tm, d), lambda b, j: (b, j, 0))]
        + [_const_spec(c.shape) for c in consts],
        out_specs=pl.BlockSpec((None, tm, d), lambda b, j: (b, j, 0)),
        out_shape=jax.ShapeDtypeStruct((bsz, seq, d), F32),
        scratch_shapes=[
            pltpu.VMEM((f_tile // LANES, tm + CARRY, LANES), F32),
            pltpu.VMEM((CARRY, f_dim), F32),
            pltpu.VMEM((tm, f_dim), BF16),
        ],
        compiler_params=pltpu.CompilerParams(
            dimension_semantics=("arbitrary", "arbitrary"), vmem_limit_bytes=VMEM_LIMIT),
        name="ffn",
    )(x, *consts)


def kernel(x, w_in, b_gate, lambda_q1, lambda_k1, lambda_q2, lambda_k2, subln_gain, w_attn_proj, conv_dw_w, conv_dw_b, conv_ln_g, conv_ln_b, w_conv_proj, b_conv_proj, w_out, ln1_g, ln1_b, w_ffn_in, ffn_dw_w, ffn_dw_b, w_ffn_down, ln2_g, ln2_b):
    bsz, seq, d = x.shape
    depth = w_in.shape[0]
    width = w_attn_proj.shape[1]
    ch = w_conv_proj.shape[1]
    f_dim = w_ffn_down.shape[1]
    assert width == ATTN_HEADS * HEAD_WIDTH and ch == width
    assert seq % TOKEN_TILE == 0 and seq % Q_TILE == 0 and bsz % ATTN_BATCH == 0
    assert f_dim % (2 * LANES) == 0
    alpha = (2.0 * depth) ** 0.25
    row = lambda a: a.reshape(1, -1)

    for l in range(depth):
        lambda_init = 0.8 - 0.6 * math.exp(-0.3 * l)
        n_proj = 3 * width + 2 * ch
        w_qkv = w_in[l][:, :3 * width].astype(BF16)
        w_glu = w_in[l][:, 3 * width:n_proj].reshape(d, 2, ch // LANES, LANES)
        w_glu = w_glu.transpose(0, 2, 1, 3).reshape(d, 2 * ch).astype(BF16)
        w_gate = w_in[l][:, n_proj:].astype(BF16)
        q0, q1, k, v = _in_proj(x.reshape(bsz * seq, d), w_qkv, width)
        shape3 = (bsz, seq, width)
        lam_params = jnp.stack([lambda_q1[l], lambda_k1[l], lambda_q2[l], lambda_k2[l]])
        attn = _diff_attn(q0.reshape(shape3), q1.reshape(shape3), k.reshape(shape3),
                          v.reshape(shape3), lam_params, row(subln_gain[l]), lambda_init)
        x = _mix(x, attn, w_glu,
                 w_gate, row(b_gate[l]), w_attn_proj[l].astype(BF16),
                 conv_dw_w[l], row(conv_dw_b[l]), row(conv_ln_g[l]), row(conv_ln_b[l]),
                 w_conv_proj[l].astype(BF16), row(b_conv_proj[l]), w_out[l].astype(BF16),
                 row(ln1_g[l]), row(ln1_b[l]), alpha)
        x = _ffn(x, w_ffn_in[l][:, :f_dim].astype(BF16), w_ffn_in[l][:, f_dim:].astype(BF16),
                 ffn_dw_w[l], row(ffn_dw_b[l]),
                 w_ffn_down[l].astype(BF16), row(ln2_g[l]), row(ln2_b[l]), alpha)
    return x
```

```python
import functools
import math

import jax
import jax.numpy as jnp
from jax import lax
from jax.experimental import pallas as pl
from jax.experimental.pallas import tpu as pltpu

CHUNK = 64
ATTN_HEADS = 4
HEAD_DIM = 64
HEAD_WIDTH = 2 * HEAD_DIM
LN_EPS = 1e-5

LANES = 128
Q_TILE = 256
ATTN_BATCH = 2
TOKEN_TILE = 1024
HALO = 32
CARRY = 8
CONV_ROWS = 128
ROW_CHUNK = 256
VMEM_LIMIT = 56 * 1024 * 1024

BF16 = jnp.bfloat16
F32 = jnp.float32


def _dot(a, b):
    return jnp.dot(a, b, preferred_element_type=F32)


def _dot_nt(a, b):
    return lax.dot_general(a, b, (((1,), (1,)), ((), ())), preferred_element_type=F32)


def _sigmoid(x):
    return 0.5 + 0.5 * jnp.tanh(0.5 * x)


def _layer_norm(x, g, b):
    mu = jnp.mean(x, axis=-1, keepdims=True)
    xc = x - mu
    var = jnp.mean(xc * xc, axis=-1, keepdims=True)
    return xc * lax.rsqrt(var + LN_EPS) * g + b


def _const_spec(shape):
    nd = len(shape)
    return pl.BlockSpec(shape, lambda *_: (0,) * nd, pipeline_mode=pl.Buffered(1))


def _start_sequence(ext_ref, tm):
    ext_ref[:, tm:, :] = jnp.zeros((ext_ref.shape[0], HALO, LANES), F32)


def _glu_conv_slab(xb, wglu_ref, cw_ref, cb_ref, ext_ref, s, conv_width):
    tm = xb.shape[0]
    base = HALO - (conv_width - 1)
    pair = _dot(xb, wglu_ref[:, 2 * s * LANES:2 * (s + 1) * LANES])
    ext_ref[s, 0:HALO, :] = ext_ref[s, tm:, :]
    ext_ref[s, HALO:, :] = pair[:, 0:LANES] * _sigmoid(pair[:, LANES:])
    lanes = slice(s * LANES, (s + 1) * LANES)
    blocks = []
    for r0 in range(0, tm, CONV_ROWS):
        acc = ext_ref[s, base + r0:base + r0 + CONV_ROWS, :] * cw_ref[0:1, lanes] + cb_ref[:, lanes]
        for k in range(1, conv_width):
            acc = acc + (ext_ref[s, base + k + r0:base + k + r0 + CONV_ROWS, :]
                         * cw_ref[k:k + 1, lanes])
        blocks.append(acc)
    return jnp.concatenate(blocks, axis=0)


def _in_proj_kernel(x_ref, w_ref, q0_ref, q1_ref, k_ref, v_ref, *, width, scale):
    xb = x_ref[...].astype(BF16)
    q = (_dot(xb, w_ref[:, 0:width]) * scale).astype(BF16)
    lane = lax.broadcasted_iota(jnp.int32, q.shape, 1)
    first = (lane & (HEAD_WIDTH - 1)) < HEAD_DIM
    zero = jnp.zeros_like(q)
    q0_ref[...] = jnp.where(first, q, zero)
    q1_ref[...] = jnp.where(first, zero, q)
    k_ref[...] = _dot(xb, w_ref[:, width:2 * width]).astype(BF16)
    v_ref[...] = _dot(xb, w_ref[:, 2 * width:3 * width]).astype(BF16)


def _in_proj(x2d, w, width):
    m, d = x2d.shape
    tm = TOKEN_TILE
    out = jax.ShapeDtypeStruct((m, width), BF16)
    row_spec = pl.BlockSpec((tm, width), lambda i: (i, 0))
    return pl.pallas_call(
        functools.partial(_in_proj_kernel, width=width, scale=HEAD_DIM ** -0.5),
        grid=(m // tm,),
        in_specs=[pl.BlockSpec((tm, d), lambda i: (i, 0)), _const_spec(w.shape)],
        out_specs=[row_spec] * 4,
        out_shape=[out] * 4,
        compiler_params=pltpu.CompilerParams(
            dimension_semantics=("arbitrary",), vmem_limit_bytes=VMEM_LIMIT),
        name="in_proj",
    )(x2d, w)


def _head_slope(h):
    return jnp.exp2(jnp.full((1, 1), -8.0 / ATTN_HEADS, F32) * (h + 1).astype(F32))


def _alibi_kernel(qaug_ref, kaug_ref):
    slope = _head_slope(pl.program_id(0))
    row = lax.broadcasted_iota(jnp.int32, qaug_ref.shape, 0)
    lane = lax.broadcasted_iota(jnp.int32, qaug_ref.shape, 1)
    hi = (row & ~255).astype(F32)
    lo = (row & 255).astype(F32)
    tail = jnp.where(jnp.logical_and(lane >= 2, lane < 4), slope, 0.0)
    qaug_ref[...] = jnp.where(lane == 0, -hi * slope,
                              jnp.where(lane == 1, -lo * slope, tail)).astype(BF16)
    one = jnp.where(lane < 2, 1.0, 0.0)
    kaug_ref[...] = jnp.where(lane == 2, hi, jnp.where(lane == 3, lo, one)).astype(BF16)


def _alibi_tables(heads, seq):
    return pl.pallas_call(
        _alibi_kernel,
        grid=(heads,),
        out_specs=[pl.BlockSpec((None, seq, LANES), lambda h: (h, 0, 0)),
                   pl.BlockSpec((seq, LANES), lambda h: (0, 0))],
        out_shape=[jax.ShapeDtypeStruct((heads, seq, LANES), BF16),
                   jax.ShapeDtypeStruct((seq, LANES), BF16)],
        compiler_params=pltpu.CompilerParams(dimension_semantics=("arbitrary",)),
        name="alibi_tables",
    )()


def _attn_kernel(lam_ref, gain_ref, qaug_ref, kaug_ref, q0_ref, q1_ref, k_ref, v_ref, o_ref,
                 *, seq, lambda_init):
    slope = _head_slope(pl.program_id(0))

    lp = lam_ref[...]
    dot1 = jnp.sum(lp[0:1, :] * lp[1:2, :], axis=-1, keepdims=True)
    dot2 = jnp.sum(lp[2:3, :] * lp[3:4, :], axis=-1, keepdims=True)
    lam = jnp.exp(dot1) - jnp.exp(dot2) + lambda_init
    gain = gain_ref[...] * (1.0 - lambda_init)

    r = lax.broadcasted_iota(jnp.int32, (Q_TILE, Q_TILE), 0)
    c = lax.broadcasted_iota(jnp.int32, (Q_TILE, Q_TILE), 1)
    ahead = jnp.maximum(c - r, 0).astype(F32)
    diag_fix = jnp.where((c // CHUNK) <= (r // CHUNK), -2.0 * slope * ahead, -jnp.inf)

    diag_fix = jnp.concatenate([diag_fix, diag_fix], axis=0)

    def scores(b, i):
        rows, keys = slice(i * Q_TILE, (i + 1) * Q_TILE), slice(0, (i + 1) * Q_TILE)
        aug = qaug_ref[rows, :]
        lhs = jnp.concatenate([jnp.concatenate([q0_ref[b, rows, :], aug], axis=1),
                               jnp.concatenate([q1_ref[b, rows, :], aug], axis=1)], axis=0)
        rhs = jnp.concatenate([k_ref[b, keys, :], kaug_ref[keys, :]], axis=1)
        s = _dot_nt(lhs, rhs)
        s_d = s[:, i * Q_TILE:] + diag_fix
        return s_d if i == 0 else jnp.concatenate([s[:, :i * Q_TILE], s_d], axis=1)

    def weighted_values(b, i, s):
        e = jnp.exp(s - jnp.max(s, axis=-1, keepdims=True))
        keys = (i + 1) * Q_TILE
        v_one = jnp.concatenate([v_ref[b, 0:keys, :], jnp.ones((keys, LANES), BF16)], axis=1)
        return _dot(e.astype(BF16), v_one)

    n_blk = seq // Q_TILE
    order = [i for pair in zip(reversed(range(n_blk)), range(n_blk)) for i in pair][:n_blk]
    items = [(b, i) for b in range(o_ref.shape[0]) for i in order]
    s_next = scores(*items[0])
    for n, (b, i) in enumerate(items):
        s_cur = s_next
        if n + 1 < len(items):
            s_next = scores(*items[n + 1])
        acc = weighted_values(b, i, s_cur)
        a0, a1 = acc[:Q_TILE, :], acc[Q_TILE:, :]
        d0 = a0[:, HEAD_WIDTH:HEAD_WIDTH + 1]
        d1 = a1[:, HEAD_WIDTH:HEAD_WIDTH + 1]
        o = a0[:, :HEAD_WIDTH] * (1.0 / d0) - a1[:, :HEAD_WIDTH] * (lam / d1)
        o = o * lax.rsqrt(jnp.mean(o * o, axis=-1, keepdims=True) + LN_EPS) * gain
        o_ref[b, i * Q_TILE:(i + 1) * Q_TILE, :] = o.astype(BF16)


def _diff_attn(q0, q1, k, v, lam_params, gain, lambda_init):
    bsz, seq, width = k.shape
    heads = width // HEAD_WIDTH
    q_aug, k_aug = _alibi_tables(heads, seq)
    blk = pl.BlockSpec((ATTN_BATCH, seq, HEAD_WIDTH), lambda h, b: (b, 0, h))
    return pl.pallas_call(
        functools.partial(_attn_kernel, seq=seq, lambda_init=lambda_init),
        grid=(heads, bsz // ATTN_BATCH),
        in_specs=[_const_spec(lam_params.shape), _const_spec(gain.shape),
                  pl.BlockSpec((None, seq, LANES), lambda h, b: (h, 0, 0)),
                  _const_spec(k_aug.shape), blk, blk, blk, blk],
        out_specs=blk,
        out_shape=jax.ShapeDtypeStruct((bsz, seq, width), BF16),
        compiler_params=pltpu.CompilerParams(
            dimension_semantics=("arbitrary", "arbitrary"), vmem_limit_bytes=VMEM_LIMIT),
        name="diff_attn",
    )(lam_params, gain, q_aug, k_aug, q0, q1, k, v)


def _mix_kernel(x_ref, attn_ref, wglu_ref, wg_ref, bg_ref, wa_ref, cw_ref, cb_ref,
                cg_ref, cbeta_ref, wc_ref, bc_ref, wo_ref, g1_ref, b1_ref, o_ref,
                ext_ref, gates_ref, attnp_ref, *, alpha, conv_width):
    tm, d = x_ref.shape
    n_chunk = gates_ref.shape[0]
    g_cols = wg_ref.shape[1] // n_chunk
    a_cols = wa_ref.shape[1] // n_chunk

    @pl.when(pl.program_id(1) == 0)
    def _():
        _start_sequence(ext_ref, tm)

    xb = x_ref[...].astype(BF16)
    slabs = []
    for s in range(n_chunk):
        slabs.append(_glu_conv_slab(xb, wglu_ref, cw_ref, cb_ref, ext_ref, s, conv_width))
        gsl = slice(s * g_cols, (s + 1) * g_cols)
        gates_ref[s] = _sigmoid(_dot(xb, wg_ref[:, gsl]) + bg_ref[:, gsl])
        asl = slice(s * a_cols, (s + 1) * a_cols)
        attnp_ref[s] = _dot(attn_ref[...], wa_ref[:, asl])

    cat = lambda ref, lo, hi: jnp.concatenate([ref[s] for s in range(lo, hi)], axis=1)
    z = _layer_norm(jnp.concatenate(slabs, axis=1), cg_ref[...], cbeta_ref[...])
    z = z * _sigmoid(z)
    conv = _dot(z.astype(BF16), wc_ref[...]) + bc_ref[...]
    half = n_chunk // 2
    mixed = cat(gates_ref, 0, half) * cat(attnp_ref, 0, n_chunk) + cat(gates_ref, half, n_chunk) * conv
    y = _dot(mixed.astype(BF16), wo_ref[...])
    o_ref[...] = _layer_norm(alpha * x_ref[...] + y, g1_ref[...], b1_ref[...])


def _mix(x, attn, wglu, wg, bg, wa, cw, cb, cg, cbeta, wc, bc, wo, g1, b1, alpha):
    bsz, seq, d = x.shape
    tm = TOKEN_TILE
    n_chunk = cw.shape[1] // LANES
    consts = (wglu, wg, bg, wa, cw, cb, cg, cbeta, wc, bc, wo, g1, b1)
    tile = lambda width: pl.BlockSpec((None, tm, width), lambda b, j: (b, j, 0))
    return pl.pallas_call(
        functools.partial(_mix_kernel, alpha=alpha, conv_width=cw.shape[0]),
        grid=(bsz, seq // tm),
        in_specs=[tile(d), tile(attn.shape[-1])] + [_const_spec(c.shape) for c in consts],
        out_specs=tile(d),
        out_shape=jax.ShapeDtypeStruct((bsz, seq, d), F32),
        scratch_shapes=[
            pltpu.VMEM((cw.shape[1] // LANES, tm + HALO, LANES), F32),
            pltpu.VMEM((n_chunk, tm, wg.shape[1] // n_chunk), F32),
            pltpu.VMEM((n_chunk, tm, wa.shape[1] // n_chunk), F32),
        ],
        compiler_params=pltpu.CompilerParams(
            dimension_semantics=("arbitrary", "arbitrary"), vmem_limit_bytes=VMEM_LIMIT),
        name="mix",
    )(x, attn, *consts)


def _ffn_kernel(x_ref, wgate_ref, wval_ref, dw_ref, db_ref, wdown_ref, g2_ref, b2_ref,
                o_ref, ext_ref, carry_ref, hid_ref, *, alpha, conv_width, f_tile):
    j = pl.program_id(1)
    tm, _ = x_ref.shape
    f_dim = hid_ref.shape[-1]

    @pl.when(j == 0)
    def _():
        carry_ref[...] = jnp.zeros_like(carry_ref)

    x = x_ref[...]
    xb = x.astype(BF16)
    base = CARRY - (conv_width - 1)
    for c0 in range(0, f_dim, f_tile):
        cols = slice(c0, c0 + f_tile)
        gate = _dot(xb, wgate_ref[:, cols])
        slabs = []
        for s in range(ext_ref.shape[0]):
            lanes = slice(c0 + s * LANES, c0 + (s + 1) * LANES)
            g_s = gate[:, s * LANES:(s + 1) * LANES]
            ext_ref[s, 0:CARRY, :] = carry_ref[:, lanes]
            ext_ref[s, CARRY:, :] = g_s
            carry_ref[:, lanes] = g_s[tm - CARRY:, :]
            a_s = ext_ref[s, base:base + tm, :] * dw_ref[0:1, lanes] + db_ref[:, lanes]
            for t in range(1, conv_width):
                a_s = a_s + ext_ref[s, base + t:base + t + tm, :] * dw_ref[t:t + 1, lanes]
            slabs.append(a_s)
        acc = jnp.concatenate(slabs, axis=1)
        act = 0.5 * acc * (1.0 + lax.erf(acc * math.sqrt(0.5)))
        val = _dot(xb, wval_ref[:, cols])
        hid_ref[:, cols] = (act * val).astype(BF16)
    for r0 in range(0, tm, ROW_CHUNK):
        rows = slice(r0, r0 + ROW_CHUNK)
        f = _dot(hid_ref[rows, :], wdown_ref[...])
        o_ref[rows, :] = _layer_norm(alpha * x_ref[rows, :] + f, g2_ref[...], b2_ref[...])


def _ffn(x, wgate, wval, dw, db, wdown, g2, b2, alpha):
    bsz, seq, d = x.shape
    f_dim = wgate.shape[-1]
    tm = TOKEN_TILE
    f_tile = 2 * LANES
    consts = (wgate, wval, dw, db, wdown, g2, b2)
    return pl.pallas_call(
        functools.partial(_ffn_kernel, alpha=alpha, conv_width=dw.shape[0], f_tile=f_tile),
        grid=(bsz, seq // tm),
        in_specs=[pl.BlockSpec((None, tm, d), lambda b, j: (b, j, 0))]
        + [_const_spec(c.shape) for c in consts],
        out_specs=pl.BlockSpec((None, tm, d), lambda b, j: (b, j, 0)),
        out_shape=jax.ShapeDtypeStruct((bsz, seq, d), F32),
        scratch_shapes=[
            pltpu.VMEM((f_tile // LANES, tm + CARRY, LANES), F32),
            pltpu.VMEM((CARRY, f_dim), F32),
            pltpu.VMEM((tm, f_dim), BF16),
        ],
        compiler_params=pltpu.CompilerParams(
            dimension_semantics=("arbitrary", "arbitrary"), vmem_limit_bytes=VMEM_LIMIT),
        name="ffn",
    )(x, *consts)


def kernel(x, w_in, b_gate, lambda_q1, lambda_k1, lambda_q2, lambda_k2, subln_gain, w_attn_proj, conv_dw_w, conv_dw_b, conv_ln_g, conv_ln_b, w_conv_proj, b_conv_proj, w_out, ln1_g, ln1_b, w_ffn_in, ffn_dw_w, ffn_dw_b, w_ffn_down, ln2_g, ln2_b):
    bsz, seq, d = x.shape
    depth = w_in.shape[0]
    width = w_attn_proj.shape[1]
    ch = w_conv_proj.shape[1]
    f_dim = w_ffn_down.shape[1]
    assert width == ATTN_HEADS * HEAD_WIDTH and ch == width
    assert seq % TOKEN_TILE == 0 and seq % Q_TILE == 0 and bsz % ATTN_BATCH == 0
    assert f_dim % (2 * LANES) == 0
    alpha = (2.0 * depth) ** 0.25
    row = lambda a: a.reshape(1, -1)

    for l in range(depth):
        lambda_init = 0.8 - 0.6 * math.exp(-0.3 * l)
        n_proj = 3 * width + 2 * ch
        w_qkv = w_in[l][:, :3 * width].astype(BF16)
        w_glu = w_in[l][:, 3 * width:n_proj].reshape(d, 2, ch // LANES, LANES)
        w_glu = w_glu.transpose(0, 2, 1, 3).reshape(d, 2 * ch).astype(BF16)
        w_gate = w_in[l][:, n_proj:].astype(BF16)
        q0, q1, k, v = _in_proj(x.reshape(bsz * seq, d), w_qkv, width)
        shape3 = (bsz, seq, width)
        lam_params = jnp.stack([lambda_q1[l], lambda_k1[l], lambda_q2[l], lambda_k2[l]])
        attn = _diff_attn(q0.reshape(shape3), q1.reshape(shape3), k.reshape(shape3),
                          v.reshape(shape3), lam_params, row(subln_gain[l]), lambda_init)
        x = _mix(x, attn, w_glu,
                 w_gate, row(b_gate[l]), w_attn_proj[l].astype(BF16),
                 conv_dw_w[l], row(conv_dw_b[l]), row(conv_ln_g[l]), row(conv_ln_b[l]),
                 w_conv_proj[l].astype(BF16), row(b_conv_proj[l]), w_out[l].astype(BF16),
                 row(ln1_g[l]), row(ln1_b[l]), alpha)
        x = _ffn(x, w_ffn_in[l][:, :f_dim].astype(BF16), w_ffn_in[l][:, f_dim:].astype(BF16),
                 ffn_dw_w[l], row(ffn_dw_b[l]),
                 w_ffn_down[l].astype(BF16), row(ln2_g[l]), row(ln2_b[l]), alpha)
    return x
```
